```python
import math
import jax, jax.numpy as jnp
from jax import lax
import numpy as np

D_MODEL = 2048
BATCH = 2
SEQ = 16384
DEPTH = 1
DEC_BATCH = 16
DEC_SEQ = 2048
PAST_LEN = 128

HEAD_DIM = 64
N_HEADS = D_MODEL // HEAD_DIM
N_HEADS_A = N_HEADS // 2
N_HEADS_B = N_HEADS - N_HEADS_A
D_A = N_HEADS_A * HEAD_DIM
D_B = N_HEADS_B * HEAD_DIM
DILATED_PATTERNS = ((128, 1), (512, 4), (2048, 16))
GRID_W = 64
NA_ROWS = 8
NA_COLS = 16
N_EXPERTS = 32
TOP_K = 4
D_FF = D_MODEL
SWIGLU_LIMIT = 7.0
SWIGLU_ALPHA = 1.702
PLE_DIM = 256
MOE_BLOCK = 512
DEEPNORM_ALPHA = (2.0 * DEPTH) ** 0.25
DEEPNORM_BETA = (8.0 * DEPTH) ** -0.25
LN_EPS = 1e-5
RMS_EPS = 1e-6
NEG_INF = -1e30

kernel_name = 'hymba_dilated_natten_moe_encoder'


def layer_norm(x, g, b):
    xf = x.astype(jnp.float32)
    mu = jnp.mean(xf, axis=-1, keepdims=True)
    xc = xf - mu
    var = jnp.mean(xc * xc, axis=-1, keepdims=True)
    return (xc * lax.rsqrt(var + LN_EPS) * g.astype(jnp.float32) + b.astype(jnp.float32)).astype(x.dtype)


def rms_norm(x, g):
    xf = x.astype(jnp.float32)
    ms = jnp.mean(xf * xf, axis=-1, keepdims=True)
    return (xf * lax.rsqrt(ms + RMS_EPS) * g.astype(jnp.float32)).astype(x.dtype)


def alibi_slopes(n):
    return 2.0 ** (-8.0 * jnp.arange(1, n + 1, dtype=jnp.float32) / n)


def dilated_branch(q, k, v, slopes, window, dilation):
    b, h, t, hd = q.shape
    half = window // (2 * dilation)
    length = t // dilation
    n_blk = -(-length // half)
    padded = n_blk * half

    def to_sub(a):
        a = a.reshape(b, h, length, dilation, hd).transpose(0, 1, 3, 2, 4)
        a = jnp.pad(a, ((0, 0), (0, 0), (0, 0), (0, padded - length), (0, 0)))
        return a.reshape(b, h, dilation, n_blk, half, hd)

    def band(a):
        a = jnp.pad(a, ((0, 0), (0, 0), (0, 0), (1, 1), (0, 0), (0, 0)))
        return jnp.concatenate([a[:, :, :, :-2], a[:, :, :, 1:-1], a[:, :, :, 2:]], axis=4)

    qs = to_sub(q)
    kb = band(to_sub(k))
    vb = band(to_sub(v))
    s = jnp.einsum('bhrnqd,bhrnkd->bhrnqk', qs, kb).astype(jnp.float32) * (HEAD_DIM ** -0.5)
    blk = jnp.arange(n_blk)[:, None]
    qi = blk * half + jnp.arange(half)[None, :]
    kj = (blk - 1) * half + jnp.arange(3 * half)[None, :]
    dist = jnp.abs(qi[:, :, None] - kj[:, None, :])
    valid = (dist <= half) & (kj[:, None, :] >= 0) & (kj[:, None, :] < length)
    alibi = slopes[:, None, None, None, None] * (dilation * dist).astype(jnp.float32)
    s = jnp.where(valid, s - alibi, NEG_INF)
    m = jnp.max(s, axis=-1)
    e = jnp.exp(s - m[..., None])
    l = jnp.sum(e, axis=-1)
    o = jnp.einsum('bhrnqk,bhrnkd->bhrnqd', e, vb.astype(jnp.float32)) / l[..., None]

    def from_sub(a):
        tail = a.shape[5:]
        a = a.reshape((b, h, dilation, padded) + tail)[:, :, :, :length]
        a = jnp.moveaxis(a, 2, 3)
        return a.reshape((b, h, t) + tail)

    return from_sub(o), from_sub(m), from_sub(l)


def dilated_mixture_attention(q, k, v, slopes):
    branches = [dilated_branch(q, k, v, slopes, w, d) for (w, d) in DILATED_PATTERNS]
    o_all = jnp.stack([br[0] for br in branches])
    m_all = jnp.stack([br[1] for br in branches])
    l_all = jnp.stack([br[2] for br in branches])
    wts = l_all * jnp.exp(m_all - jnp.max(m_all, axis=0, keepdims=True))
    return jnp.einsum('pbht,pbhtd->bhtd', wts, o_all) / jnp.sum(wts, axis=0)[..., None]


def neighbourhood_attention(q, k, v, rpb):
    b, h, t, hd = q.shape
    rows = t // GRID_W
    kh = min(NA_ROWS, rows)
    q5 = q.reshape(b, h, rows, GRID_W, hd)
    k5 = k.reshape(b, h, rows, GRID_W, hd)
    v5 = v.reshape(b, h, rows, GRID_W, hd)
    c = jnp.arange(GRID_W)
    col_start = jnp.clip(c - NA_COLS // 2, 0, GRID_W - NA_COLS)
    col_ok = (c[None, :] >= col_start[:, None]) & (c[None, :] < col_start[:, None] + NA_COLS)
    col_off = jnp.clip(c[None, :] - c[:, None], -(NA_COLS - 1), NA_COLS - 1) + (NA_COLS - 1)

    def one_row(r):
        r0 = jnp.clip(r - kh // 2, 0, rows - kh)
        qr = lax.dynamic_index_in_dim(q5, r, axis=2, keepdims=False)
        kr = lax.dynamic_slice_in_dim(k5, r0, kh, axis=2)
        vr = lax.dynamic_slice_in_dim(v5, r0, kh, axis=2)
        s = jnp.einsum('bhqd,bhikd->bhqik', qr, kr).astype(jnp.float32) * (HEAD_DIM ** -0.5)
        row_off = r0 + jnp.arange(kh) - r + (NA_ROWS - 1)
        bias = rpb[:, row_off[None, :, None], col_off[:, None, :]].astype(jnp.float32)
        s = jnp.where(col_ok[:, None, :], s + bias, NEG_INF)
        p = jax.nn.softmax(s.reshape(b, h, GRID_W, kh * GRID_W), axis=-1).reshape(s.shape)
        return jnp.einsum('bhqik,bhikd->bhqd', p, vr.astype(jnp.float32))

    o = lax.map(one_row, jnp.arange(rows))
    return o.transpose(1, 2, 0, 3, 4).reshape(b, h, t, hd)


def token_mixer(x, w_in, out_norm_a, out_norm_b, na_rpb, w_out, slopes):
    b, t, _ = x.shape
    qkv = x @ w_in
    qa, ka, va, qb, kb, vb = jnp.split(
        qkv, [D_A, 2 * D_A, 3 * D_A, 3 * D_A + D_B, 3 * D_A + 2 * D_B], axis=-1)

    def heads(a, n):
        return a.reshape(b, t, n, HEAD_DIM).transpose(0, 2, 1, 3)

    def merge(a):
        return a.transpose(0, 2, 1, 3).reshape(b, t, -1).astype(x.dtype)

    ya = merge(dilated_mixture_attention(heads(qa, N_HEADS_A), heads(ka, N_HEADS_A), heads(va, N_HEADS_A), slopes))
    yb = merge(neighbourhood_attention(heads(qb, N_HEADS_B), heads(kb, N_HEADS_B), heads(vb, N_HEADS_B), na_rpb))
    y = jnp.concatenate([rms_norm(ya, out_norm_a), rms_norm(yb, out_norm_b)], axis=-1)
    return y @ w_out


def moe_ffn(x, w_router, b_router, w1, b1, w2, b2):
    bsz, t, d = x.shape
    xt = x.reshape(-1, d)
    n = xt.shape[0]
    logits = (xt @ w_router + b_router).astype(jnp.float32)
    top_val, top_idx = lax.top_k(logits, TOP_K)
    gates = jax.nn.softmax(top_val, axis=-1)
    a = n * TOP_K
    flat_e = top_idx.reshape(-1)
    flat_g = gates.reshape(-1)
    flat_tok = jnp.arange(a, dtype=jnp.int32) // TOP_K
    order = jnp.argsort(flat_e)
    e_sorted = flat_e[order]
    counts = jnp.bincount(flat_e, length=N_EXPERTS)
    start = jnp.cumsum(counts) - counts
    padded = (counts + MOE_BLOCK - 1) // MOE_BLOCK * MOE_BLOCK
    pad_end = jnp.cumsum(padded)
    pad_start = pad_end - padded
    slot = pad_start[e_sorted] + jnp.arange(a, dtype=jnp.int32) - start[e_sorted]
    n_blocks = -(-a // MOE_BLOCK) + N_EXPERTS
    n_slots = n_blocks * MOE_BLOCK
    slot_tok = jnp.zeros((n_slots,), jnp.int32).at[slot].set(flat_tok[order])
    slot_gate = jnp.zeros((n_slots,), jnp.float32).at[slot].set(flat_g[order])
    block_expert = jnp.minimum(
        jnp.searchsorted(pad_end, jnp.arange(n_blocks) * MOE_BLOCK, side='right'), N_EXPERTS - 1)

    def run_block(args):
        tok, e = args
        hid = xt[tok] @ w1[e] + b1[e]
        glu, lin = jnp.split(hid, 2, axis=-1)
        glu = jnp.minimum(glu, SWIGLU_LIMIT)
        lin = jnp.clip(lin, -SWIGLU_LIMIT, SWIGLU_LIMIT)
        act = glu * jax.nn.sigmoid(SWIGLU_ALPHA * glu) * (lin + 1.0)
        return act @ w2[e] + b2[e]

    yb = lax.map(run_block, (slot_tok.reshape(n_blocks, MOE_BLOCK), block_expert))
    yb = yb.reshape(n_slots, d)
    y = jnp.zeros_like(xt).at[slot_tok].add((yb * slot_gate[:, None].astype(yb.dtype)).astype(xt.dtype))
    return y.reshape(bsz, t, d)


def trunk(x, p, w_in, out_norm_a, out_norm_b, na_rpb, w_out, ln1_g, ln1_b,
          w_router, b_router, w1, b1, w2, b2, ln2_g, ln2_b,
          w_ple, w_ple_gate, b_ple_gate, ple_norm_g):
    slopes = alibi_slopes(N_HEADS_A)
    for i in range(DEPTH):
        h = token_mixer(x, w_in[i], out_norm_a[i], out_norm_b[i], na_rpb[i], w_out[i], slopes)
        x = layer_norm(DEEPNORM_ALPHA * x + h, ln1_g[i], ln1_b[i])
        h = moe_ffn(x, w_router[i], b_router[i], w1[i], b1[i], w2[i], b2[i])
        x = layer_norm(DEEPNORM_ALPHA * x + h, ln2_g[i], ln2_b[i])
        ple = (p[i] @ w_ple[i]) * jax.nn.sigmoid(x @ w_ple_gate[i] + b_ple_gate[i])
        x = x + rms_norm(ple, ple_norm_g[i])
    return x


def setup_inputs(seed: int = 0) -> dict:
    key = jax.random.key(seed)
    ks = jax.random.split(key, 32)
    f32 = jnp.float32

    def nrm(k, shape, scale):
        return jax.random.normal(k, shape, f32) * scale

    L = DEPTH
    return {
        'x_prompt': nrm(ks[0], (BATCH, SEQ, D_MODEL), 1.0),
        'x_sample': nrm(ks[1], (DEC_BATCH, DEC_SEQ, D_MODEL), 1.0),
        'p_prompt': nrm(ks[2], (DEPTH, BATCH, SEQ, PLE_DIM), 1.0),
        'p_sample': nrm(ks[3], (DEPTH, DEC_BATCH, DEC_SEQ, PLE_DIM), 1.0),
        'w_in': nrm(ks[4], (L, D_MODEL, 3 * D_MODEL), D_MODEL ** -0.5),
        'out_norm_a': 1.0 + nrm(ks[5], (L, D_A), 0.01),
        'out_norm_b': 1.0 + nrm(ks[6], (L, D_B), 0.01),
        'na_rpb': nrm(ks[7], (L, N_HEADS_B, 2 * NA_ROWS - 1, 2 * NA_COLS - 1), 0.1),
        'w_out': nrm(ks[8], (L, D_MODEL, D_MODEL), D_MODEL ** -0.5 * DEEPNORM_BETA),
        'ln1_g': 1.0 + nrm(ks[9], (L, D_MODEL), 0.01),
        'ln1_b': nrm(ks[10], (L, D_MODEL), 0.01),
        'w_router': nrm(ks[11], (L, D_MODEL, N_EXPERTS), D_MODEL ** -0.5),
        'b_router': nrm(ks[12], (L, N_EXPERTS), 0.01),
        'w1': nrm(ks[13], (L, N_EXPERTS, D_MODEL, 2 * D_FF), D_MODEL ** -0.5),
        'b1': nrm(ks[14], (L, N_EXPERTS, 2 * D_FF), 0.01),
        'w2': nrm(ks[15], (L, N_EXPERTS, D_FF, D_MODEL), D_FF ** -0.5 * DEEPNORM_BETA),
        'b2': nrm(ks[16], (L, N_EXPERTS, D_MODEL), 0.01),
        'ln2_g': 1.0 + nrm(ks[17], (L, D_MODEL), 0.01),
        'ln2_b': nrm(ks[18], (L, D_MODEL), 0.01),
        'w_ple': nrm(ks[19], (L, PLE_DIM, D_MODEL), PLE_DIM ** -0.5),
        'w_ple_gate': nrm(ks[20], (L, D_MODEL, D_MODEL), D_MODEL ** -0.5),
        'b_ple_gate': nrm(ks[21], (L, D_MODEL), 0.01),
        'ple_norm_g': 1.0 + nrm(ks[22], (L, D_MODEL), 0.01),
    }


def reference(x_prompt, x_sample, p_prompt, p_sample, w_in, out_norm_a, out_norm_b, na_rpb, w_out,
              ln1_g, ln1_b, w_router, b_router, w1, b1, w2, b2, ln2_g, ln2_b,
              w_ple, w_ple_gate, b_ple_gate, ple_norm_g):
    y_prompt = trunk(x_prompt, p_prompt, w_in, out_norm_a, out_norm_b, na_rpb, w_out, ln1_g, ln1_b,
                     w_router, b_router, w1, b1, w2, b2, ln2_g, ln2_b,
                     w_ple, w_ple_gate, b_ple_gate, ple_norm_g)
    y_sample = trunk(x_sample, p_sample, w_in, out_norm_a, out_norm_b, na_rpb, w_out, ln1_g, ln1_b,
                     w_router, b_router, w1, b1, w2, b2, ln2_g, ln2_b,
                     w_ple, w_ple_gate, b_ple_gate, ple_norm_g)
    return (y_prompt, y_sample)
```

```python
import functools

import jax
import jax.numpy as jnp
import numpy as np
from jax import lax
from jax.experimental import pallas as pl
from jax.experimental.pallas import tpu as pltpu

F32 = jnp.float32
BF16 = jnp.bfloat16

D_MODEL = 2048
HEAD_DIM = 64
N_HEADS_A = 16
N_HEADS_B = 16
D_A = N_HEADS_A * HEAD_DIM
D_B = N_HEADS_B * HEAD_DIM
DILATIONS = (1, 4, 16)
BAND_HALF = 64
GRID_W = 64
NA_ROWS = 8
NA_COLS = 16
N_EXPERTS = 32
TOP_K = 4
D_FF = D_MODEL
SWIGLU_LIMIT = 7.0
SWIGLU_ALPHA = 1.702
PLE_DIM = 256
DEEPNORM_ALPHA = 2.0 ** 0.25
LN_EPS = 1e-5
RMS_EPS = 1e-6
NEG_INF = -1e30

LANES = 128
HEAD_PAIRS_A = D_A // LANES
HEAD_PAIRS_B = D_B // LANES
ATT_BLOCK = 1024
NA_GROUP_ROWS = 4
NA_UNION_ROWS = 12
NA_HALO = 256
MOE_ROWS = 512
MOE_FT = 512
VMEM_LIMIT = 56 * 1024 * 1024


def _qkv_kernel(x_ref, wa_ref, wb_ref, oa_ref, ob_ref, xb_ref):
    @pl.when(pl.program_id(1) == 0)
    def _():
        xb_ref[...] = x_ref[...].astype(BF16)

    xb = xb_ref[...]
    oa_ref[...] = jnp.dot(xb, wa_ref[...], preferred_element_type=F32)
    ob_ref[...] = jnp.dot(xb, wb_ref[...], preferred_element_type=F32).astype(BF16)


def _qkv_proj(x2d, wa, wb, tm=512, tn=512):
    n = x2d.shape[0]
    ca, cb = wa.shape[1], wb.shape[1]
    assert ca == cb and n % tm == 0 and ca % tn == 0
    return pl.pallas_call(
        _qkv_kernel,
        grid=(n // tm, ca // tn),
        in_specs=[
            pl.BlockSpec((tm, D_MODEL), lambda i, j: (i, 0)),
            pl.BlockSpec((D_MODEL, tn), lambda i, j: (0, j)),
            pl.BlockSpec((D_MODEL, tn), lambda i, j: (0, j)),
        ],
        out_specs=[
            pl.BlockSpec((tm, tn), lambda i, j: (i, j)),
            pl.BlockSpec((tm, tn), lambda i, j: (i, j)),
        ],
        out_shape=[jax.ShapeDtypeStruct((n, ca), F32), jax.ShapeDtypeStruct((n, cb), BF16)],
        scratch_shapes=[pltpu.VMEM((tm, D_MODEL), BF16)],
        compiler_params=pltpu.CompilerParams(
            dimension_semantics=("parallel", "arbitrary"), vmem_limit_bytes=VMEM_LIMIT),
        name="qkv_proj",
    )(x2d, wa, wb)


def _dilated_bias_tables():
    slopes = 2.0 ** (-8.0 * jnp.arange(1, N_HEADS_A + 1, dtype=F32) / N_HEADS_A)
    qi = jnp.arange(128)[:, None]
    kj = jnp.arange(256)[None, :] - BAND_HALF
    dist = jnp.abs(qi - kj)
    in_band = dist <= BAND_HALF
    tabs = []
    for d in DILATIONS:
        alibi = slopes[:, None, None] * (d * dist).astype(F32)[None]
        t = jnp.where(in_band[None], -alibi, NEG_INF)
        if 128 * d > ATT_BLOCK:
            t = t.reshape(HEAD_PAIRS_A, 2, 128, 256)[:, :, :64]
            t = jnp.concatenate([t.reshape(HEAD_PAIRS_A, 128, 256),
                                 jnp.full((HEAD_PAIRS_A, 128, 256), NEG_INF, F32)], axis=1)
        else:
            t = t.reshape(HEAD_PAIRS_A, 256, 256)
        tabs.append(t)
    return jnp.stack(tabs, axis=1)


def _attn_a_kernel(q_ref, kp_ref, kc_ref, kn_ref, vp_ref, vc_ref, vn_ref, bias_ref, o_ref,
                   kwin, vwin, acc_ref, m_ref, l_ref, *, seq_len):
    qb = pl.program_id(2)
    blk = ATT_BLOCK
    kwin[0:blk, :] = kp_ref[0]
    kwin[blk:2 * blk, :] = kc_ref[0]
    kwin[2 * blk:3 * blk, :] = kn_ref[0]
    vwin[0:blk, :] = vp_ref[0]
    vwin[blk:2 * blk, :] = vc_ref[0]
    vwin[2 * blk:3 * blk, :] = vn_ref[0]
    is_h0 = lax.broadcasted_iota(jnp.int32, (1, LANES), 1) < HEAD_DIM

    for p, d in enumerate(DILATIONS):
        lq = blk // d
        qs = min(128, lq)
        ks = qs + 2 * BAND_HALF
        nsb = lq // qs
        sub_len = seq_len // d

        def body(it, carry, d=d, qs=qs, ks=ks, nsb=nsb, lq=lq, sub_len=sub_len, p=p):
            r = it // nsb
            sb = it % nsb
            bias = bias_ref[0, p, 0:2 * qs, 0:ks]
            col = lax.broadcasted_iota(jnp.int32, (1, ks), 1)
            q0 = r + d * sb * qs
            k0 = blk - BAND_HALF * d + q0
            if d == 1:
                q = q_ref[0, pl.ds(q0, qs), :]
                k = kwin[pl.ds(k0, ks), :]
                v = vwin[pl.ds(k0, ks), :]
            else:
                q = q_ref[0, pl.ds(q0, qs, stride=d), :]
                k = kwin[pl.ds(k0, ks, stride=d), :]
                v = vwin[pl.ds(k0, ks, stride=d), :]
            q = q * (HEAD_DIM ** -0.5)
            q2 = jnp.concatenate([jnp.where(is_h0, q, 0.0), jnp.where(is_h0, 0.0, q)], axis=0).astype(BF16)
            s = lax.dot_general(q2, k.astype(BF16), (((1,), (1,)), ((), ())),
                                preferred_element_type=F32)
            kidx = qb * lq + sb * qs - BAND_HALF + col
            ok = (kidx >= 0) & (kidx < sub_len)
            s = jnp.where(ok, s + bias, NEG_INF)
            m = jnp.max(s, axis=-1, keepdims=True)
            e = jnp.exp(s - m)
            l = jnp.sum(e, axis=-1, keepdims=True)
            o2 = jnp.dot(e.astype(BF16), v.astype(BF16), preferred_element_type=F32)
            o = jnp.where(is_h0, o2[:qs], o2[qs:])
            mm = jnp.where(is_h0, m[:qs], m[qs:])
            ll = jnp.where(is_h0, l[:qs], l[qs:])
            if d == 1:
                acc_ref[p, pl.ds(q0, qs), :] = o
                m_ref[p, pl.ds(q0, qs), :] = mm
                l_ref[p, pl.ds(q0, qs), :] = ll
            else:
                acc_ref[p, pl.ds(q0, qs, stride=d), :] = o
                m_ref[p, pl.ds(q0, qs, stride=d), :] = mm
                l_ref[p, pl.ds(q0, qs, stride=d), :] = ll
            return carry

        lax.fori_loop(0, d * nsb, body, 0)

    m_all = jnp.maximum(jnp.maximum(m_ref[0], m_ref[1]), m_ref[2])
    num = jnp.zeros((blk, LANES), F32)
    den = jnp.zeros((blk, LANES), F32)
    for p in range(len(DILATIONS)):
        a = jnp.exp(m_ref[p] - m_all)
        num = num + a * acc_ref[p]
        den = den + a * l_ref[p]
    o_ref[0] = num / den


def _attn_a(qkv_a, bias_tab):
    b, t, _ = qkv_a.shape
    blk = ATT_BLOCK
    nqb = t // blk
    assert t % blk == 0
    hp = HEAD_PAIRS_A

    def spec(col0, shift):
        if shift == 0:
            return pl.BlockSpec((1, blk, LANES), lambda h, bi, qi: (bi, qi, col0 + h))
        if shift < 0:
            return pl.BlockSpec((1, blk, LANES), lambda h, bi, qi: (bi, jnp.maximum(qi - 1, 0), col0 + h))
        return pl.BlockSpec((1, blk, LANES), lambda h, bi, qi: (bi, jnp.minimum(qi + 1, nqb - 1), col0 + h))

    return pl.pallas_call(
        functools.partial(_attn_a_kernel, seq_len=t),
        grid=(hp, b, nqb),
        in_specs=[
            spec(0, 0),
            spec(hp, -1), spec(hp, 0), spec(hp, 1),
            spec(2 * hp, -1), spec(2 * hp, 0), spec(2 * hp, 1),
            pl.BlockSpec((1, 3, 256, 256), lambda h, bi, qi: (h, 0, 0, 0)),
        ],
        out_specs=pl.BlockSpec((1, blk, LANES), lambda h, bi, qi: (bi, qi, h)),
        out_shape=jax.ShapeDtypeStruct((b, t, D_A), F32),
        scratch_shapes=[
            pltpu.VMEM((3 * blk, LANES), F32),
            pltpu.VMEM((3 * blk, LANES), F32),
            pltpu.VMEM((3, blk, LANES), F32),
            pltpu.VMEM((3, blk, LANES), F32),
            pltpu.VMEM((3, blk, LANES), F32),
        ],
        compiler_params=pltpu.CompilerParams(
            dimension_semantics=("parallel", "parallel", "parallel"), vmem_limit_bytes=VMEM_LIMIT),
        name="attn_dilated",
    )(qkv_a, qkv_a, qkv_a, qkv_a, qkv_a, qkv_a, qkv_a, bias_tab)


def _na_bias_tables(rpb):
    g, u = NA_GROUP_ROWS, NA_UNION_ROWS
    c = np.arange(GRID_W)
    col_start = np.clip(c - NA_COLS // 2, 0, GRID_W - NA_COLS)
    col_ok = (c[None, :] >= col_start[:, None]) & (c[None, :] < col_start[:, None] + NA_COLS)
    col_off = np.clip(c[None, :] - c[:, None], -(NA_COLS - 1), NA_COLS - 1) + (NA_COLS - 1)
    rho = np.arange(g)[:, None]
    kap = np.arange(u)[None, :]
    variants = []
    for r0_rel, delta in ((np.zeros((g, 1), np.int64), kap - rho),
                          (rho, kap - 4 - rho),
                          (np.full((g, 1), 4), kap - 8 - rho)):
        row_ok = (kap >= r0_rel) & (kap < r0_rel + NA_ROWS)
        row_off = np.clip(delta + NA_ROWS - 1, 0, 2 * NA_ROWS - 2)
        tab = rpb[:, row_off[:, None, :, None], col_off[None, :, None, :]]
        ok = row_ok[:, None, :, None] & col_ok[None, :, None, :]
        tab = jnp.where(ok[None], tab.astype(F32), NEG_INF)
        variants.append(tab.reshape(N_HEADS_B, g * GRID_W, u * GRID_W))
    t = jnp.stack(variants, axis=1)
    t = t.reshape(HEAD_PAIRS_B, 2, 3, g * GRID_W, u * GRID_W).transpose(0, 2, 1, 3, 4)
    return t.reshape(HEAD_PAIRS_B, 3, 2 * g * GRID_W, u * GRID_W)


def _attn_b_kernel(q_ref, kp_ref, kc_ref, kn_ref, vp_ref, vc_ref, vn_ref, bias_ref, o_ref,
                   kwin, vwin, *, seq_len):
    qb = pl.program_id(2)
    blk = ATT_BLOCK
    halo = NA_HALO
    kwin[0:halo, :] = kp_ref[0, blk - halo:blk, :]
    kwin[halo:halo + blk, :] = kc_ref[0]
    kwin[halo + blk:2 * halo + blk, :] = kn_ref[0, 0:halo, :]
    vwin[0:halo, :] = vp_ref[0, blk - halo:blk, :]
    vwin[halo:halo + blk, :] = vc_ref[0]
    vwin[halo + blk:2 * halo + blk, :] = vn_ref[0, 0:halo, :]
    is_h0 = lax.broadcasted_iota(jnp.int32, (1, LANES), 1) < HEAD_DIM
    rows = seq_len // GRID_W
    rows_per_blk = blk // GRID_W
    gq = NA_GROUP_ROWS * GRID_W
    uk = NA_UNION_ROWS * GRID_W
    n_groups = rows_per_blk // NA_GROUP_ROWS

    def body(gi, carry):
        rg = qb * rows_per_blk + gi * NA_GROUP_ROWS
        u0 = jnp.clip(rg - NA_ROWS // 2, 0, rows - NA_UNION_ROWS)
        variant = jnp.where(rg == 0, 0, jnp.where(rg == rows - NA_GROUP_ROWS, 2, 1))
        start = pl.multiple_of((u0 - qb * rows_per_blk) * GRID_W + halo, GRID_W)
        q0 = pl.multiple_of(gi * gq, gq)
        q = q_ref[0, pl.ds(q0, gq), :].astype(F32) * (HEAD_DIM ** -0.5)
        q2 = jnp.concatenate([jnp.where(is_h0, q, 0.0), jnp.where(is_h0, 0.0, q)], axis=0).astype(BF16)
        k = kwin[pl.ds(start, uk), :]
        v = vwin[pl.ds(start, uk), :]
        s = lax.dot_general(q2, k, (((1,), (1,)), ((), ())), preferred_element_type=F32)
        s = s + bias_ref[0, variant]
        m = jnp.max(s, axis=-1, keepdims=True)
        e = jnp.exp(s - m)
        l = jnp.sum(e, axis=-1, keepdims=True)
        o2 = jnp.dot(e.astype(BF16), v, preferred_element_type=F32)
        o = jnp.where(is_h0, o2[:gq], o2[gq:])
        ll = jnp.where(is_h0, l[:gq], l[gq:])
        o_ref[0, pl.ds(q0, gq), :] = o / ll
        return carry

    lax.fori_loop(0, n_groups, body, 0)


def _attn_b(qkv_b, bias_tab):
    b, t, _ = qkv_b.shape
    blk = ATT_BLOCK
    nqb = t // blk
    assert t % blk == 0 and (t // GRID_W) >= NA_UNION_ROWS
    hp = HEAD_PAIRS_B

    def spec(col0, shift):
        if shift == 0:
            return pl.BlockSpec((1, blk, LANES), lambda h, bi, qi: (bi, qi, col0 + h))
        if shift < 0:
            return pl.BlockSpec((1, blk, LANES), lambda h, bi, qi: (bi, jnp.maximum(qi - 1, 0), col0 + h))
        return pl.BlockSpec((1, blk, LANES), lambda h, bi, qi: (bi, jnp.minimum(qi + 1, nqb - 1), col0 + h))

    gq2 = 2 * NA_GROUP_ROWS * GRID_W
    uk = NA_UNION_ROWS * GRID_W
    return pl.pallas_call(
        functools.partial(_attn_b_kernel, seq_len=t),
        grid=(hp, b, nqb),
        in_specs=[
            spec(0, 0),
            spec(hp, -1), spec(hp, 0), spec(hp, 1),
            spec(2 * hp, -1), spec(2 * hp, 0), spec(2 * hp, 1),
            pl.BlockSpec((1, 3, gq2, uk), lambda h, bi, qi: (h, 0, 0, 0)),
        ],
        out_specs=pl.BlockSpec((1, blk, LANES), lambda h, bi, qi: (bi, qi, h)),
        out_shape=jax.ShapeDtypeStruct((b, t, D_B), F32),
        scratch_shapes=[
            pltpu.VMEM((blk + 2 * NA_HALO, LANES), BF16),
            pltpu.VMEM((blk + 2 * NA_HALO, LANES), BF16),
        ],
        compiler_params=pltpu.CompilerParams(
            dimension_semantics=("parallel", "parallel", "parallel"), vmem_limit_bytes=VMEM_LIMIT),
        name="attn_neighbourhood",
    )(qkv_b, qkv_b, qkv_b, qkv_b, qkv_b, qkv_b, qkv_b, bias_tab)


def _layer_norm(z, g, b):
    mu = jnp.mean(z, axis=-1, keepdims=True)
    zc = z - mu
    var = jnp.mean(zc * zc, axis=-1, keepdims=True)
    return zc * lax.rsqrt(var + LN_EPS) * g + b


def _rms_norm(y, g):
    ms = jnp.mean(y * y, axis=-1, keepdims=True)
    return y * lax.rsqrt(ms + RMS_EPS) * g


def _outproj_kernel(ya_ref, yb_ref, x_ref, wo_ref, ga_ref, gb_ref, lg_ref, lb_ref, wr2_ref, wrh_ref, br_ref,
                    x1_ref, topi_ref, topg_ref):
    na = _rms_norm(ya_ref[...], ga_ref[...]).astype(BF16)
    nb = _rms_norm(yb_ref[...], gb_ref[...]).astype(BF16)
    h = jnp.dot(na, wo_ref[0:D_A, :], preferred_element_type=F32)
    h = h + jnp.dot(nb, wo_ref[D_A:D_A + D_B, :], preferred_element_type=F32)
    x1 = _layer_norm(DEEPNORM_ALPHA * x_ref[...] + h, lg_ref[...], lb_ref[...])
    x1_ref[...] = x1
    hi = x1.astype(BF16)
    lo = (x1 - hi.astype(F32)).astype(BF16)
    l2 = jnp.dot(hi, wr2_ref[...], preferred_element_type=F32)
    logits = l2[:, :N_EXPERTS] + l2[:, N_EXPERTS:] + br_ref[...]
    logits = logits + jnp.dot(lo, wrh_ref[...], preferred_element_type=F32)
    lane = lax.broadcasted_iota(jnp.int32, logits.shape, 1)
    work = logits
    vals, idxs = [], []
    for _ in range(TOP_K):
        m = jnp.max(work, axis=-1, keepdims=True)
        idx = jnp.min(jnp.where(work == m, lane, N_EXPERTS), axis=-1, keepdims=True)
        vals.append(m)
        idxs.append(idx)
        work = jnp.where(lane == idx, -jnp.inf, work)
    es = [jnp.exp(v - vals[0]) for v in vals]
    tot = es[0] + es[1] + es[2] + es[3]
    topi_ref[...] = jnp.concatenate(idxs, axis=1)
    topg_ref[...] = jnp.concatenate([e / tot for e in es], axis=1)


def _outproj(ya2d, yb2d, x2d, wo, ga, gb, lg, lb, wr2, wrh, br, tm=256):
    n = x2d.shape[0]
    assert n % tm == 0
    row = lambda w: pl.BlockSpec((tm, w), lambda i: (i, 0))
    full = lambda a: pl.BlockSpec(a.shape, lambda i: (0,) * a.ndim)
    return pl.pallas_call(
        _outproj_kernel,
        grid=(n // tm,),
        in_specs=[row(D_A), row(D_B), row(D_MODEL), full(wo), full(ga), full(gb), full(lg), full(lb),
                  full(wr2), full(wrh), full(br)],
        out_specs=[row(D_MODEL), row(TOP_K), row(TOP_K)],
        out_shape=[jax.ShapeDtypeStruct((n, D_MODEL), F32),
                   jax.ShapeDtypeStruct((n, TOP_K), jnp.int32),
                   jax.ShapeDtypeStruct((n, TOP_K), F32)],
        compiler_params=pltpu.CompilerParams(
            dimension_semantics=("parallel",), vmem_limit_bytes=VMEM_LIMIT),
        name="outproj_ln_router",
    )(ya2d, yb2d, x2d, wo, ga, gb, lg, lb, wr2, wrh, br)


def _moe_kernel(be_ref, nv_ref, tok_ref, x_hbm, w1g_ref, w1l_ref, b1g_ref, b1l_ref, w2_ref, b2_ref,
                out_ref, xs_ref, xsb_ref, sem):
    i = pl.program_id(0)
    j = pl.program_id(1)
    valid = i < nv_ref[0]
    bm = xs_ref.shape[0]

    @pl.when(jnp.logical_and(valid, j == 0))
    def _():
        def issue(k, c):
            tok = tok_ref[0, 0, k]
            pltpu.make_async_copy(x_hbm.at[tok], xs_ref.at[k], sem).start()
            return c

        lax.fori_loop(0, bm, issue, 0)
        pltpu.make_async_copy(x_hbm.at[pl.ds(0, bm)], xs_ref, sem).wait()
        xsb_ref[...] = xs_ref[...].astype(BF16)

    @pl.when(valid)
    def _():
        xsb = xsb_ref[...]
        glu = jnp.dot(xsb, w1g_ref[0], preferred_element_type=F32) + b1g_ref[0]
        lin = jnp.dot(xsb, w1l_ref[0], preferred_element_type=F32) + b1l_ref[0]
        glu = jnp.minimum(glu, SWIGLU_LIMIT)
        lin = jnp.clip(lin, -SWIGLU_LIMIT, SWIGLU_LIMIT)
        act = glu * jax.nn.sigmoid(SWIGLU_ALPHA * glu) * (lin + 1.0)
        contrib = jnp.dot(act.astype(BF16), w2_ref[0], preferred_element_type=F32)

        @pl.when(j == 0)
        def _():
            out_ref[...] = contrib + b2_ref[0]

        @pl.when(j != 0)
        def _():
            out_ref[...] += contrib

    @pl.when(jnp.logical_and(jnp.logical_not(valid), j == 0))
    def _():
        out_ref[...] = jnp.zeros_like(out_ref)


def _moe_ffn(block_expert, n_valid, slot_tok, x1, w1b, b1r, w2b, b2r):
    n_blocks, _, bm = slot_tok.shape
    ft = MOE_FT
    nf = D_FF // ft
    grid_spec = pltpu.PrefetchScalarGridSpec(
        num_scalar_prefetch=2,
        grid=(n_blocks, nf),
        in_specs=[
            pl.BlockSpec((1, 1, bm), lambda i, j, be, nv: (i, 0, 0), memory_space=pltpu.SMEM),
            pl.BlockSpec(memory_space=pl.ANY),
            pl.BlockSpec((1, D_MODEL, ft), lambda i, j, be, nv: (be[i], 0, j)),
            pl.BlockSpec((1, D_MODEL, ft), lambda i, j, be, nv: (be[i], 0, nf + j)),
            pl.BlockSpec((1, 1, ft), lambda i, j, be, nv: (be[i], 0, j)),
            pl.BlockSpec((1, 1, ft), lambda i, j, be, nv: (be[i], 0, nf + j)),
            pl.BlockSpec((1, ft, D_MODEL), lambda i, j, be, nv: (be[i], j, 0)),
            pl.BlockSpec((1, 1, D_MODEL), lambda i, j, be, nv: (be[i], 0, 0)),
        ],
        out_specs=pl.BlockSpec((bm, D_MODEL), lambda i, j, be, nv: (i, 0)),
        scratch_shapes=[
            pltpu.VMEM((bm, D_MODEL), F32),
            pltpu.VMEM((bm, D_MODEL), BF16),
            pltpu.SemaphoreType.DMA(()),
        ],
    )
    return pl.pallas_call(
        _moe_kernel,
        grid_spec=grid_spec,
        out_shape=jax.ShapeDtypeStruct((n_blocks * bm, D_MODEL), F32),
        compiler_params=pltpu.CompilerParams(
            dimension_semantics=("arbitrary", "arbitrary"), vmem_limit_bytes=VMEM_LIMIT),
        name="moe_ffn",
    )(block_expert, n_valid, slot_tok, x1, w1b, w1b, b1r, b1r, w2b, b2r)


def _final_kernel(pos_ref, yb_hbm, gate_ref, x1_ref, p_ref, lg_ref, lb_ref, wg_ref, bg_ref, wp_ref, pg_ref,
                  out_ref, buf_ref, sem):
    tm = x1_ref.shape[0]

    def issue(t, c):
        for k in range(TOP_K):
            pos = pos_ref[0, 0, TOP_K * t + k]
            pltpu.make_async_copy(yb_hbm.at[pos], buf_ref.at[k, t], sem).start()
        return c

    lax.fori_loop(0, tm, issue, 0)
    for k in range(TOP_K):
        pltpu.make_async_copy(yb_hbm.at[pl.ds(0, tm)], buf_ref.at[k], sem).wait()
    g = gate_ref[...]
    y = buf_ref[0] * g[:, 0:1]
    for k in range(1, TOP_K):
        y = y + buf_ref[k] * g[:, k:k + 1]
    x2 = _layer_norm(DEEPNORM_ALPHA * x1_ref[...] + y, lg_ref[...], lb_ref[...])
    gate = jax.nn.sigmoid(jnp.dot(x2.astype(BF16), wg_ref[...], preferred_element_type=F32) + bg_ref[...])
    ple = jnp.dot(p_ref[...].astype(BF16), wp_ref[...], preferred_element_type=F32) * gate
    out_ref[...] = x2 + _rms_norm(ple, pg_ref[...])


def _final(pos3, yb, gates, x1, row0, p2d, lg, lb, wg, bg, wp, pg, tm=256):
    n = p2d.shape[0]
    assert n % tm == 0 and row0 % tm == 0
    blk0 = row0 // tm
    full = lambda a: pl.BlockSpec(a.shape, lambda i: (0,) * a.ndim)
    return pl.pallas_call(
        _final_kernel,
        grid=(n // tm,),
        in_specs=[
            pl.BlockSpec((1, 1, TOP_K * tm), lambda i: (blk0 + i, 0, 0), memory_space=pltpu.SMEM),
            pl.BlockSpec(memory_space=pl.ANY),
            pl.BlockSpec((tm, TOP_K), lambda i: (blk0 + i, 0)),
            pl.BlockSpec((tm, D_MODEL), lambda i: (blk0 + i, 0)),
            pl.BlockSpec((tm, PLE_DIM), lambda i: (i, 0)),
            full(lg), full(lb), full(wg), full(bg), full(wp), full(pg),
        ],
        out_specs=pl.BlockSpec((tm, D_MODEL), lambda i: (i, 0)),
        out_shape=jax.ShapeDtypeStruct((n, D_MODEL), F32),
        scratch_shapes=[pltpu.VMEM((TOP_K, tm, D_MODEL), F32), pltpu.SemaphoreType.DMA(())],
        compiler_params=pltpu.CompilerParams(
            dimension_semantics=("arbitrary",), vmem_limit_bytes=VMEM_LIMIT),
        name="combine_ln_ple",
    )(pos3, yb, gates, x1, p2d, lg, lb, wg, bg, wp, pg)


def _dispatch_tables(topi, bm):
    n = topi.shape[0]
    a = n * TOP_K
    flat_e = topi.reshape(-1)
    onehot = (flat_e[:, None] == jnp.arange(N_EXPERTS, dtype=jnp.int32)[None, :]).astype(jnp.int32)
    csum = jnp.cumsum(onehot, axis=0)
    rank = jnp.sum(csum * onehot, axis=1) - 1
    counts = csum[-1]
    start = jnp.cumsum(counts) - counts
    nblk_e = (counts + bm - 1) // bm
    blk_end = jnp.cumsum(nblk_e)
    blk_start = blk_end - nblk_e
    pos = blk_start[flat_e] * bm + rank
    n_blocks = a // bm + N_EXPERTS
    blocks = jnp.arange(n_blocks, dtype=jnp.int32)
    block_expert = jnp.minimum(jnp.searchsorted(blk_end, blocks, side='right'), N_EXPERTS - 1).astype(jnp.int32)
    n_valid = blk_end[-1:].astype(jnp.int32)
    order = jnp.sort(flat_e * a + jnp.arange(a, dtype=jnp.int32)) % a
    src = start[block_expert] + (blocks - blk_start[block_expert]) * bm
    src = jnp.clip(src[:, None] + jnp.arange(bm, dtype=jnp.int32)[None, :], 0, a - 1)
    slot_tok = (order[src] // TOP_K).astype(jnp.int32)
    return block_expert, n_valid, slot_tok.reshape(n_blocks, 1, bm), pos.astype(jnp.int32)


def kernel(x_prompt, x_sample, p_prompt, p_sample, w_in, out_norm_a, out_norm_b, na_rpb, w_out, ln1_g, ln1_b, w_router, b_router, w1, b1, w2, b2, ln2_g, ln2_b, w_ple, w_ple_gate, b_ple_gate, ple_norm_g):
    li = 0
    w_in_b = w_in[li].astype(BF16)
    wa, wb = w_in_b[:, :3 * D_A], w_in_b[:, 3 * D_A:]
    wo = w_out[li].astype(BF16)
    wr = w_router[li]
    wr_hi = wr.astype(BF16)
    wr_lo = (wr - wr_hi.astype(F32)).astype(BF16)
    wr2 = jnp.concatenate([wr_hi, wr_lo], axis=1)
    row = lambda v: v.reshape(1, -1).astype(F32)
    bias_a = _dilated_bias_tables()
    bias_b = _na_bias_tables(na_rpb[li])

    x1s, topis, topgs = [], [], []
    for x in (x_prompt, x_sample):
        b, t, _ = x.shape
        x2d = x.reshape(b * t, D_MODEL)
        qkv_a, qkv_b = _qkv_proj(x2d, wa, wb)
        ya = _attn_a(qkv_a.reshape(b, t, 3 * D_A), bias_a)
        yb = _attn_b(qkv_b.reshape(b, t, 3 * D_B), bias_b)
        x1, topi, topg = _outproj(ya.reshape(b * t, D_A), yb.reshape(b * t, D_B), x2d, wo,
                                  row(out_norm_a[li]), row(out_norm_b[li]), row(ln1_g[li]), row(ln1_b[li]),
                                  wr2, wr_hi, row(b_router[li]))
        x1s.append(x1)
        topis.append(topi)
        topgs.append(topg)

    x1 = jnp.concatenate(x1s, axis=0)
    topi = jnp.concatenate(topis, axis=0)
    topg = jnp.concatenate(topgs, axis=0)
    block_expert, n_valid, slot_tok, pos = _dispatch_tables(topi, MOE_ROWS)
    y_slots = _moe_ffn(block_expert, n_valid, slot_tok, x1,
                       w1[li].astype(BF16), b1[li].reshape(N_EXPERTS, 1, 2 * D_FF),
                       w2[li].astype(BF16), b2[li].reshape(N_EXPERTS, 1, D_MODEL))

    tm = 256
    pos3 = pos.reshape(-1, 1, TOP_K * tm)
    wg = w_ple_gate[li].astype(BF16)
    wp = w_ple[li].astype(BF16)
    outs = []
    row0 = 0
    for x, p in ((x_prompt, p_prompt), (x_sample, p_sample)):
        b, t, _ = x.shape
        out = _final(pos3, y_slots, topg, x1, row0, p[li].reshape(b * t, PLE_DIM),
                     row(ln2_g[li]), row(ln2_b[li]), wg, row(b_ple_gate[li]), wp, row(ple_norm_g[li]), tm=tm)
        outs.append(out.reshape(b, t, D_MODEL))
        row0 += b * t
    return tuple(outs)
```

```python
import functools

import jax
import jax.numpy as jnp
import numpy as np
from jax import lax
from jax.experimental import pallas as pl
from jax.experimental.pallas import tpu as pltpu

F32 = jnp.float32
BF16 = jnp.bfloat16

D_MODEL = 2048
HEAD_DIM = 64
N_HEADS_A = 16
N_HEADS_B = 16
D_A = N_HEADS_A * HEAD_DIM
D_B = N_HEADS_B * HEAD_DIM
DILATIONS = (1, 4, 16)
BAND_HALF = 64
GRID_W = 64
NA_ROWS = 8
NA_COLS = 16
N_EXPERTS = 32
TOP_K = 4
D_FF = D_MODEL
SWIGLU_LIMIT = 7.0
SWIGLU_ALPHA = 1.702
PLE_DIM = 256
DEEPNORM_ALPHA = 2.0 ** 0.25
LN_EPS = 1e-5
RMS_EPS = 1e-6
NEG_INF = -1e30

LANES = 128
HEAD_PAIRS_A = D_A // LANES
HEAD_PAIRS_B = D_B // LANES
ATT_BLOCK = 1024
DIL_BLOCK = 2048
DIL_HALO = BAND_HALF * max(DILATIONS)
NA_GROUP_ROWS = 4
NA_UNION_ROWS = 12
NA_HALO = 256
MOE_ROWS = 512
MOE_FT = 1024
VMEM_LIMIT = 56 * 1024 * 1024


def _qkv_kernel(x_ref, wa_ref, wb_ref, oa_ref, ob_ref, xb_ref):
    @pl.when(pl.program_id(1) == 0)
    def _():
        xb_ref[...] = x_ref[...].astype(BF16)

    xb = xb_ref[...]
    oa_ref[...] = jnp.dot(xb, wa_ref[...], preferred_element_type=F32)
    ob_ref[...] = jnp.dot(xb, wb_ref[...], preferred_element_type=F32).astype(BF16)


def _qkv_proj(x2d, wa, wb, tm=512, tn=512):
    n = x2d.shape[0]
    ca, cb = wa.shape[1], wb.shape[1]
    assert ca == cb and n % tm == 0 and ca % tn == 0
    return pl.pallas_call(
        _qkv_kernel,
        grid=(n // tm, ca // tn),
        in_specs=[
            pl.BlockSpec((tm, D_MODEL), lambda i, j: (i, 0)),
            pl.BlockSpec((D_MODEL, tn), lambda i, j: (0, j)),
            pl.BlockSpec((D_MODEL, tn), lambda i, j: (0, j)),
        ],
        out_specs=[
            pl.BlockSpec((tm, tn), lambda i, j: (i, j)),
            pl.BlockSpec((tm, tn), lambda i, j: (i, j)),
        ],
        out_shape=[jax.ShapeDtypeStruct((n, ca), F32), jax.ShapeDtypeStruct((n, cb), BF16)],
        scratch_shapes=[pltpu.VMEM((tm, D_MODEL), BF16)],
        compiler_params=pltpu.CompilerParams(
            dimension_semantics=("parallel", "arbitrary"), vmem_limit_bytes=VMEM_LIMIT),
        name="qkv_proj",
    )(x2d, wa, wb)


def _dilated_bias_tables():
    slopes = 2.0 ** (-8.0 * jnp.arange(1, N_HEADS_A + 1, dtype=F32) / N_HEADS_A)
    qi = jnp.arange(128)[:, None]
    kj = jnp.arange(256)[None, :] - BAND_HALF
    dist = jnp.abs(qi - kj)
    in_band = dist <= BAND_HALF
    tabs = []
    for d in DILATIONS:
        alibi = slopes[:, None, None] * (d * dist).astype(F32)[None]
        t = jnp.where(in_band[None], -alibi, NEG_INF)
        tabs.append(t.reshape(HEAD_PAIRS_A, 256, 256))
    return jnp.stack(tabs, axis=1)


def _attn_a_kernel(q_ref, kp_ref, kc_ref, kn_ref, vp_ref, vc_ref, vn_ref, bias_ref, o_ref,
                   kwin, vwin, acc_ref, m_ref, l_ref, *, seq_len):
    qb = pl.program_id(2)
    blk = DIL_BLOCK
    halo = DIL_HALO
    kwin[0:halo, :] = kp_ref[0]
    kwin[halo:halo + blk, :] = kc_ref[0]
    kwin[halo + blk:2 * halo + blk, :] = kn_ref[0]
    vwin[0:halo, :] = vp_ref[0]
    vwin[halo:halo + blk, :] = vc_ref[0]
    vwin[halo + blk:2 * halo + blk, :] = vn_ref[0]
    is_h0 = lax.broadcasted_iota(jnp.int32, (1, LANES), 1) < HEAD_DIM

    for p, d in enumerate(DILATIONS):
        lq = blk // d
        qs = 128
        ks = qs + 2 * BAND_HALF
        nsb = lq // qs
        sub_len = seq_len // d

        def body(it, carry, d=d, qs=qs, ks=ks, nsb=nsb, lq=lq, sub_len=sub_len, p=p):
            r = it // nsb
            sb = it % nsb
            bias = bias_ref[0, p, 0:2 * qs, 0:ks]
            col = lax.broadcasted_iota(jnp.int32, (1, ks), 1)
            q0 = r + d * sb * qs
            k0 = halo - BAND_HALF * d + q0
            if d == 1:
                q = q_ref[0, pl.ds(q0, qs), :]
                k = kwin[pl.ds(k0, ks), :]
                v = vwin[pl.ds(k0, ks), :]
            else:
                q = q_ref[0, pl.ds(q0, qs, stride=d), :]
                k = kwin[pl.ds(k0, ks, stride=d), :]
                v = vwin[pl.ds(k0, ks, stride=d), :]
            q = q * (HEAD_DIM ** -0.5)
            q2 = jnp.concatenate([jnp.where(is_h0, q, 0.0), jnp.where(is_h0, 0.0, q)], axis=0).astype(BF16)
            s = lax.dot_general(q2, k.astype(BF16), (((1,), (1,)), ((), ())),
                                preferred_element_type=F32)
            kidx = qb * lq + sb * qs - BAND_HALF + col
            ok = (kidx >= 0) & (kidx < sub_len)
            s = jnp.where(ok, s + bias, NEG_INF)
            m = jnp.max(s, axis=-1, keepdims=True)
            e = jnp.exp(s - m)
            l = jnp.sum(e, axis=-1, keepdims=True)
            o2 = jnp.dot(e.astype(BF16), v.astype(BF16), preferred_element_type=F32)
            o = jnp.where(is_h0, o2[:qs], o2[qs:])
            mm = jnp.where(is_h0, m[:qs], m[qs:])
            ll = jnp.where(is_h0, l[:qs], l[qs:])
            if d == 1:
                acc_ref[p, pl.ds(q0, qs), :] = o
                m_ref[p, pl.ds(q0, qs), :] = mm
                l_ref[p, pl.ds(q0, qs), :] = ll
            else:
                acc_ref[p, pl.ds(q0, qs, stride=d), :] = o
                m_ref[p, pl.ds(q0, qs, stride=d), :] = mm
                l_ref[p, pl.ds(q0, qs, stride=d), :] = ll
            return carry

        lax.fori_loop(0, d * nsb, body, 0, unroll=8)

    m_all = jnp.maximum(jnp.maximum(m_ref[0], m_ref[1]), m_ref[2])
    num = jnp.zeros((blk, LANES), F32)
    den = jnp.zeros((blk, LANES), F32)
    for p in range(len(DILATIONS)):
        a = jnp.exp(m_ref[p] - m_all)
        num = num + a * acc_ref[p]
        den = den + a * l_ref[p]
    o_ref[0] = num / den


def _attn_a(qkv_a, bias_tab):
    b, t, _ = qkv_a.shape
    blk = DIL_BLOCK
    halo = DIL_HALO
    nqb = t // blk
    assert t % blk == 0 and blk % halo == 0
    hp = HEAD_PAIRS_A
    per = blk // halo
    last = t // halo - 1

    def spec(col0, shift):
        if shift == 0:
            return pl.BlockSpec((1, blk, LANES), lambda h, bi, qi: (bi, qi, col0 + h))
        if shift < 0:
            return pl.BlockSpec((1, halo, LANES), lambda h, bi, qi: (bi, jnp.maximum(qi * per - 1, 0), col0 + h))
        return pl.BlockSpec((1, halo, LANES), lambda h, bi, qi: (bi, jnp.minimum((qi + 1) * per, last), col0 + h))

    return pl.pallas_call(
        functools.partial(_attn_a_kernel, seq_len=t),
        grid=(hp, b, nqb),
        in_specs=[
            spec(0, 0),
            spec(hp, -1), spec(hp, 0), spec(hp, 1),
            spec(2 * hp, -1), spec(2 * hp, 0), spec(2 * hp, 1),
            pl.BlockSpec((1, 3, 256, 256), lambda h, bi, qi: (h, 0, 0, 0)),
        ],
        out_specs=pl.BlockSpec((1, blk, LANES), lambda h, bi, qi: (bi, qi, h)),
        out_shape=jax.ShapeDtypeStruct((b, t, D_A), F32),
        scratch_shapes=[
            pltpu.VMEM((blk + 2 * halo, LANES), F32),
            pltpu.VMEM((blk + 2 * halo, LANES), F32),
            pltpu.VMEM((3, blk, LANES), F32),
            pltpu.VMEM((3, blk, LANES), F32),
            pltpu.VMEM((3, blk, LANES), F32),
        ],
        compiler_params=pltpu.CompilerParams(
            dimension_semantics=("parallel", "parallel", "parallel"), vmem_limit_bytes=VMEM_LIMIT),
        name="attn_dilated",
    )(qkv_a, qkv_a, qkv_a, qkv_a, qkv_a, qkv_a, qkv_a, bias_tab)


def _na_bias_tables(rpb):
    g, u = NA_GROUP_ROWS, NA_UNION_ROWS
    c = np.arange(GRID_W)
    col_start = np.clip(c - NA_COLS // 2, 0, GRID_W - NA_COLS)
    col_ok = (c[None, :] >= col_start[:, None]) & (c[None, :] < col_start[:, None] + NA_COLS)
    col_off = np.clip(c[None, :] - c[:, None], -(NA_COLS - 1), NA_COLS - 1) + (NA_COLS - 1)
    onehot = (col_off[None] == np.arange(2 * NA_COLS - 1)[:, None, None]).astype(np.float32)
    tab15 = jnp.einsum('hrc,cqk->hrqk', rpb.astype(F32), onehot, precision=lax.Precision.HIGHEST)
    tab15 = jnp.where(col_ok[None, None], tab15, NEG_INF)
    masked = jnp.full((N_HEADS_B, GRID_W, GRID_W), NEG_INF, F32)
    variants = []
    for r0_rel, delta0 in ((lambda rho: 0, 0),
                           (lambda rho: rho, -4),
                           (lambda rho: 4, -8)):
        q_rows = []
        for rho in range(g):
            tiles = []
            for kap in range(u):
                row_ok = r0_rel(rho) <= kap < r0_rel(rho) + NA_ROWS
                row_off = kap - rho + delta0 + NA_ROWS - 1
                tiles.append(tab15[:, row_off] if row_ok else masked)
            q_rows.append(jnp.concatenate(tiles, axis=-1))
        variants.append(jnp.concatenate(q_rows, axis=1))
    t = jnp.stack(variants, axis=1)
    t = t.reshape(HEAD_PAIRS_B, 2, 3, g * GRID_W, u * GRID_W).transpose(0, 2, 1, 3, 4)
    return t.reshape(HEAD_PAIRS_B, 3, 2 * g * GRID_W, u * GRID_W)


def _attn_b_kernel(q_ref, kp_ref, kc_ref, kn_ref, vp_ref, vc_ref, vn_ref, bias_ref, o_ref,
                   kwin, vwin, *, seq_len):
    qb = pl.program_id(2)
    blk = ATT_BLOCK
    halo = NA_HALO
    kwin[0:halo, :] = kp_ref[0]
    kwin[halo:halo + blk, :] = kc_ref[0]
    kwin[halo + blk:2 * halo + blk, :] = kn_ref[0]
    vwin[0:halo, :] = vp_ref[0]
    vwin[halo:halo + blk, :] = vc_ref[0]
    vwin[halo + blk:2 * halo + blk, :] = vn_ref[0]
    is_h0 = lax.broadcasted_iota(jnp.int32, (1, LANES), 1) < HEAD_DIM
    rows = seq_len // GRID_W
    rows_per_blk = blk // GRID_W
    gq = NA_GROUP_ROWS * GRID_W
    uk = NA_UNION_ROWS * GRID_W
    n_groups = rows_per_blk // NA_GROUP_ROWS

    def body(gi, carry):
        rg = qb * rows_per_blk + gi * NA_GROUP_ROWS
        u0 = jnp.clip(rg - NA_ROWS // 2, 0, rows - NA_UNION_ROWS)
        variant = jnp.where(rg == 0, 0, jnp.where(rg == rows - NA_GROUP_ROWS, 2, 1))
        start = pl.multiple_of((u0 - qb * rows_per_blk) * GRID_W + halo, GRID_W)
        q0 = pl.multiple_of(gi * gq, gq)
        q = q_ref[0, pl.ds(q0, gq), :].astype(F32) * (HEAD_DIM ** -0.5)
        q2 = jnp.concatenate([jnp.where(is_h0, q, 0.0), jnp.where(is_h0, 0.0, q)], axis=0).astype(BF16)
        k = kwin[pl.ds(start, uk), :]
        v = vwin[pl.ds(start, uk), :]
        s = lax.dot_general(q2, k, (((1,), (1,)), ((), ())), preferred_element_type=F32)
        s = s + bias_ref[0, variant]
        m = jnp.max(s, axis=-1, keepdims=True)
        e = jnp.exp(s - m)
        l = jnp.sum(e, axis=-1, keepdims=True)
        o2 = jnp.dot(e.astype(BF16), v, preferred_element_type=F32)
        o = jnp.where(is_h0, o2[:gq], o2[gq:])
        ll = jnp.where(is_h0, l[:gq], l[gq:])
        o_ref[0, pl.ds(q0, gq), :] = o / ll
        return carry

    lax.fori_loop(0, n_groups, body, 0, unroll=2)


def _attn_b(qkv_b, bias_tab):
    b, t, _ = qkv_b.shape
    blk = ATT_BLOCK
    halo = NA_HALO
    nqb = t // blk
    assert t % blk == 0 and blk % halo == 0 and (t // GRID_W) >= NA_UNION_ROWS
    hp = HEAD_PAIRS_B
    per = blk // halo
    last = t // halo - 1

    def spec(col0, shift):
        if shift == 0:
            return pl.BlockSpec((1, blk, LANES), lambda h, bi, qi: (bi, qi, col0 + h))
        if shift < 0:
            return pl.BlockSpec((1, halo, LANES), lambda h, bi, qi: (bi, jnp.maximum(qi * per - 1, 0), col0 + h))
        return pl.BlockSpec((1, halo, LANES), lambda h, bi, qi: (bi, jnp.minimum((qi + 1) * per, last), col0 + h))

    gq2 = 2 * NA_GROUP_ROWS * GRID_W
    uk = NA_UNION_ROWS * GRID_W
    return pl.pallas_call(
        functools.partial(_attn_b_kernel, seq_len=t),
        grid=(hp, b, nqb),
        in_specs=[
            spec(0, 0),
            spec(hp, -1), spec(hp, 0), spec(hp, 1),
            spec(2 * hp, -1), spec(2 * hp, 0), spec(2 * hp, 1),
            pl.BlockSpec((1, 3, gq2, uk), lambda h, bi, qi: (h, 0, 0, 0)),
        ],
        out_specs=pl.BlockSpec((1, blk, LANES), lambda h, bi, qi: (bi, qi, h)),
        out_shape=jax.ShapeDtypeStruct((b, t, D_B), F32),
        scratch_shapes=[
            pltpu.VMEM((blk + 2 * NA_HALO, LANES), BF16),
            pltpu.VMEM((blk + 2 * NA_HALO, LANES), BF16),
        ],
        compiler_params=pltpu.CompilerParams(
            dimension_semantics=("parallel", "parallel", "parallel"), vmem_limit_bytes=VMEM_LIMIT),
        name="attn_neighbourhood",
    )(qkv_b, qkv_b, qkv_b, qkv_b, qkv_b, qkv_b, qkv_b, bias_tab)


def _layer_norm(z, g, b):
    mu = jnp.mean(z, axis=-1, keepdims=True)
    zc = z - mu
    var = jnp.mean(zc * zc, axis=-1, keepdims=True)
    return zc * lax.rsqrt(var + LN_EPS) * g + b


def _rms_norm(y, g):
    ms = jnp.mean(y * y, axis=-1, keepdims=True)
    return y * lax.rsqrt(ms + RMS_EPS) * g


def _outproj_kernel(ya_ref, yb_ref, x_ref, wo_ref, ga_ref, gb_ref, lg_ref, lb_ref, wr2_ref, wrh_ref, br_ref,
                    x1_ref, topi_ref, topg_ref):
    na = _rms_norm(ya_ref[...], ga_ref[...]).astype(BF16)
    nb = _rms_norm(yb_ref[...], gb_ref[...]).astype(BF16)
    h = jnp.dot(na, wo_ref[0:D_A, :], preferred_element_type=F32)
    h = h + jnp.dot(nb, wo_ref[D_A:D_A + D_B, :], preferred_element_type=F32)
    x1 = _layer_norm(DEEPNORM_ALPHA * x_ref[...] + h, lg_ref[...], lb_ref[...])
    x1_ref[...] = x1
    hi = x1.astype(BF16)
    lo = (x1 - hi.astype(F32)).astype(BF16)
    l2 = jnp.dot(hi, wr2_ref[...], preferred_element_type=F32)
    logits = l2[:, :N_EXPERTS] + l2[:, N_EXPERTS:] + br_ref[...]
    logits = logits + jnp.dot(lo, wrh_ref[...], preferred_element_type=F32)
    lane = lax.broadcasted_iota(jnp.int32, logits.shape, 1)
    work = logits
    vals, idxs = [], []
    for _ in range(TOP_K):
        m = jnp.max(work, axis=-1, keepdims=True)
        idx = jnp.min(jnp.where(work == m, lane, N_EXPERTS), axis=-1, keepdims=True)
        vals.append(m)
        idxs.append(idx)
        work = jnp.where(lane == idx, -jnp.inf, work)
    es = [jnp.exp(v - vals[0]) for v in vals]
    tot = es[0] + es[1] + es[2] + es[3]
    topi_ref[...] = jnp.concatenate(idxs, axis=1)
    topg_ref[...] = jnp.concatenate([e / tot for e in es], axis=1)


def _outproj(ya2d, yb2d, x2d, wo, ga, gb, lg, lb, wr2, wrh, br, tm=256):
    n = x2d.shape[0]
    assert n % tm == 0
    row = lambda w: pl.BlockSpec((tm, w), lambda i: (i, 0))
    full = lambda a: pl.BlockSpec(a.shape, lambda i: (0,) * a.ndim)
    return pl.pallas_call(
        _outproj_kernel,
        grid=(n // tm,),
        in_specs=[row(D_A), row(D_B), row(D_MODEL), full(wo), full(ga), full(gb), full(lg), full(lb),
                  full(wr2), full(wrh), full(br)],
        out_specs=[row(D_MODEL), row(TOP_K), row(TOP_K)],
        out_shape=[jax.ShapeDtypeStruct((n, D_MODEL), F32),
                   jax.ShapeDtypeStruct((n, TOP_K), jnp.int32),
                   jax.ShapeDtypeStruct((n, TOP_K), F32)],
        compiler_params=pltpu.CompilerParams(
            dimension_semantics=("parallel",), vmem_limit_bytes=VMEM_LIMIT),
        name="outproj_ln_router",
    )(ya2d, yb2d, x2d, wo, ga, gb, lg, lb, wr2, wrh, br)


HALF_D = D_MODEL // 2
HI_MASK = np.uint32(0xFFFF0000)


def _dispatch_kernel(zb_ref, pos_ref, xa_ref, xb_ref, xs_hbm, pk_ref, zbuf, sem, zsem, *, tiles_a):
    i = pl.program_id(0)
    tm = xa_ref.shape[0]
    bm = zbuf.shape[0]

    @pl.when(i == 0)
    def _():
        zbuf[...] = jnp.zeros_like(zbuf)

        def zero_copy(z):
            start = pl.multiple_of(zb_ref[z] * bm, bm)
            return pltpu.make_async_copy(zbuf, xs_hbm.at[pl.ds(start, bm)], zsem)

        for z in range(2 * N_EXPERTS):
            @pl.when(zb_ref[z] >= 0)
            def _():
                zero_copy(z).start()
        for z in range(2 * N_EXPERTS):
            @pl.when(zb_ref[z] >= 0)
            def _():
                zero_copy(z).wait()

    def pack(x):
        bits = lax.bitcast_convert_type(x.astype(BF16).astype(F32), jnp.uint32)
        pk_ref[...] = (bits[:, :HALF_D] & HI_MASK) | (bits[:, HALF_D:] >> 16)

    @pl.when(i < tiles_a)
    def _():
        pack(xa_ref[...])

    @pl.when(i >= tiles_a)
    def _():
        pack(xb_ref[...])

    def issue(t, c):
        for k in range(TOP_K):
            pos = pos_ref[0, 0, TOP_K * t + k]
            pltpu.make_async_copy(pk_ref.at[t], xs_hbm.at[pos], sem).start()
        return c

    lax.fori_loop(0, tm, issue, 0)
    for k in range(TOP_K):
        pltpu.make_async_copy(pk_ref, xs_hbm.at[pl.ds(0, tm)], sem).wait()


def _dispatch(zero_blocks, pos3, x1a, x1b, n_slots, bm, tm=256):
    ta, tb = x1a.shape[0] // tm, x1b.shape[0] // tm
    assert x1a.shape[0] % tm == 0 and x1b.shape[0] % tm == 0
    grid_spec = pltpu.PrefetchScalarGridSpec(
        num_scalar_prefetch=1,
        grid=(ta + tb,),
        in_specs=[
            pl.BlockSpec((1, 1, TOP_K * tm), lambda i, zb: (i, 0, 0), memory_space=pltpu.SMEM),
            pl.BlockSpec((tm, D_MODEL), lambda i, zb: (jnp.minimum(i, ta - 1), 0)),
            pl.BlockSpec((tm, D_MODEL), lambda i, zb: (jnp.maximum(i - ta, 0), 0)),
        ],
        out_specs=pl.BlockSpec(memory_space=pl.ANY),
        scratch_shapes=[
            pltpu.VMEM((tm, HALF_D), jnp.uint32),
            pltpu.VMEM((bm, HALF_D), jnp.uint32),
            pltpu.SemaphoreType.DMA(()),
            pltpu.SemaphoreType.DMA(()),
        ],
    )
    return pl.pallas_call(
        functools.partial(_dispatch_kernel, tiles_a=ta),
        grid_spec=grid_spec,
        out_shape=jax.ShapeDtypeStruct((n_slots, HALF_D), jnp.uint32),
        compiler_params=pltpu.CompilerParams(
            dimension_semantics=("arbitrary",), vmem_limit_bytes=VMEM_LIMIT),
        name="moe_dispatch",
    )(zero_blocks, pos3, x1a, x1b)


def _moe_kernel(be_ref, nv_ref, xs_ref, w1g_ref, w1l_ref, b1g_ref, b1l_ref, w2_ref, b2_ref,
                out_ref, xsb_ref):
    i = pl.program_id(0)
    j = pl.program_id(1)
    valid = i < nv_ref[0]

    @pl.when(jnp.logical_and(valid, j == 0))
    def _():
        w = xs_ref[...]
        xsb_ref[:, :HALF_D] = lax.bitcast_convert_type(w & HI_MASK, F32).astype(BF16)
        xsb_ref[:, HALF_D:] = lax.bitcast_convert_type(w << 16, F32).astype(BF16)

    @pl.when(valid)
    def _():
        xsb = xsb_ref[...]
        glu = jnp.dot(xsb, w1g_ref[0], preferred_element_type=F32) + b1g_ref[0]
        lin = jnp.dot(xsb, w1l_ref[0], preferred_element_type=F32) + b1l_ref[0]
        glu = jnp.minimum(glu, SWIGLU_LIMIT)
        lin = jnp.clip(lin, -SWIGLU_LIMIT, SWIGLU_LIMIT)
        act = glu * jax.nn.sigmoid(SWIGLU_ALPHA * glu) * (lin + 1.0)
        contrib = jnp.dot(act.astype(BF16), w2_ref[0], preferred_element_type=F32)

        @pl.when(j == 0)
        def _():
            out_ref[...] = contrib + b2_ref[0]

        @pl.when(j != 0)
        def _():
            out_ref[...] += contrib

    @pl.when(jnp.logical_and(jnp.logical_not(valid), j == 0))
    def _():
        out_ref[...] = jnp.zeros_like(out_ref)


def _moe_ffn(block_expert, n_valid, xs, w1b, b1r, w2b, b2r, bm):
    n_blocks = xs.shape[0] // bm
    ft = MOE_FT
    nf = D_FF // ft
    grid_spec = pltpu.PrefetchScalarGridSpec(
        num_scalar_prefetch=2,
        grid=(n_blocks, nf),
        in_specs=[
            pl.BlockSpec((bm, HALF_D), lambda i, j, be, nv: (jnp.minimum(i, nv[0] - 1), 0)),
            pl.BlockSpec((1, D_MODEL, ft), lambda i, j, be, nv: (be[i], 0, j)),
            pl.BlockSpec((1, D_MODEL, ft), lambda i, j, be, nv: (be[i], 0, nf + j)),
            pl.BlockSpec((1, 1, ft), lambda i, j, be, nv: (be[i], 0, j)),
            pl.BlockSpec((1, 1, ft), lambda i, j, be, nv: (be[i], 0, nf + j)),
            pl.BlockSpec((1, ft, D_MODEL), lambda i, j, be, nv: (be[i], j, 0)),
            pl.BlockSpec((1, 1, D_MODEL), lambda i, j, be, nv: (be[i], 0, 0)),
        ],
        out_specs=pl.BlockSpec((bm, D_MODEL), lambda i, j, be, nv: (i, 0)),
        scratch_shapes=[pltpu.VMEM((bm, D_MODEL), BF16)],
    )
    return pl.pallas_call(
        _moe_kernel,
        grid_spec=grid_spec,
        out_shape=jax.ShapeDtypeStruct((n_blocks * bm, D_MODEL), F32),
        compiler_params=pltpu.CompilerParams(
            dimension_semantics=("arbitrary", "arbitrary"), vmem_limit_bytes=VMEM_LIMIT),
        name="moe_ffn",
    )(block_expert, n_valid, xs, w1b, w1b, b1r, b1r, w2b, b2r)


def _final_kernel(pos_ref, yb_hbm, gate_ref, x1_ref, p_ref, lg_ref, lb_ref, wg_ref, bg_ref, wp_ref, pg_ref,
                  out_ref, buf_ref, sem):
    tm = x1_ref.shape[0]

    def issue(t, c):
        for k in range(TOP_K):
            pos = pos_ref[0, 0, TOP_K * t + k]
            pltpu.make_async_copy(yb_hbm.at[pos], buf_ref.at[k, t], sem).start()
        return c

    lax.fori_loop(0, tm, issue, 0)
    for k in range(TOP_K):
        pltpu.make_async_copy(yb_hbm.at[pl.ds(0, tm)], buf_ref.at[k], sem).wait()
    g = gate_ref[...]
    y = buf_ref[0] * g[:, 0:1]
    for k in range(1, TOP_K):
        y = y + buf_ref[k] * g[:, k:k + 1]
    x2 = _layer_norm(DEEPNORM_ALPHA * x1_ref[...] + y, lg_ref[...], lb_ref[...])
    gate = jax.nn.sigmoid(jnp.dot(x2.astype(BF16), wg_ref[...], preferred_element_type=F32) + bg_ref[...])
    ple = jnp.dot(p_ref[...].astype(BF16), wp_ref[...], preferred_element_type=F32) * gate
    out_ref[...] = x2 + _rms_norm(ple, pg_ref[...])


def _final(pos3, yb, gates, x1, p2d, lg, lb, wg, bg, wp, pg, tm=256):
    n = p2d.shape[0]
    assert n % tm == 0
    full = lambda a: pl.BlockSpec(a.shape, lambda i: (0,) * a.ndim)
    return pl.pallas_call(
        _final_kernel,
        grid=(n // tm,),
        in_specs=[
            pl.BlockSpec((1, 1, TOP_K * tm), lambda i: (i, 0, 0), memory_space=pltpu.SMEM),
            pl.BlockSpec(memory_space=pl.ANY),
            pl.BlockSpec((tm, TOP_K), lambda i: (i, 0)),
            pl.BlockSpec((tm, D_MODEL), lambda i: (i, 0)),
            pl.BlockSpec((tm, PLE_DIM), lambda i: (i, 0)),
            full(lg), full(lb), full(wg), full(bg), full(wp), full(pg),
        ],
        out_specs=pl.BlockSpec((tm, D_MODEL), lambda i: (i, 0)),
        out_shape=jax.ShapeDtypeStruct((n, D_MODEL), F32),
        scratch_shapes=[pltpu.VMEM((TOP_K, tm, D_MODEL), F32), pltpu.SemaphoreType.DMA(())],
        compiler_params=pltpu.CompilerParams(
            dimension_semantics=("arbitrary",), vmem_limit_bytes=VMEM_LIMIT),
        name="combine_ln_ple",
    )(pos3, yb, gates, x1, p2d, lg, lb, wg, bg, wp, pg)


def _dispatch_tables(topi, bm):
    n = topi.shape[0]
    a = n * TOP_K
    flat_e = topi.reshape(-1)
    onehot = (flat_e[:, None] == jnp.arange(N_EXPERTS, dtype=jnp.int32)[None, :]).astype(jnp.int32)
    csum = jnp.cumsum(onehot, axis=0)
    counts = csum[-1]
    nblk_e = (counts + bm - 1) // bm
    blk_end = jnp.cumsum(nblk_e)
    blk_start = blk_end - nblk_e
    pos = jnp.sum(onehot * (csum - 1 + (blk_start * bm)[None, :]), axis=1)
    n_blocks = a // bm + N_EXPERTS
    blocks = jnp.arange(n_blocks, dtype=jnp.int32)
    block_expert = jnp.minimum(jnp.sum((blocks[:, None] >= blk_end[None, :]).astype(jnp.int32), axis=1),
                               N_EXPERTS - 1)
    n_valid = blk_end[-1:].astype(jnp.int32)
    last_blk = jnp.where(nblk_e > 0, blk_end - 1, -1)
    trailing = n_valid[0] + jnp.arange(N_EXPERTS, dtype=jnp.int32)
    trailing = jnp.where(trailing < n_blocks, trailing, -1)
    zero_blocks = jnp.concatenate([last_blk, trailing]).astype(jnp.int32)
    return pos.astype(jnp.int32), block_expert.astype(jnp.int32), n_valid, zero_blocks, n_blocks


def kernel(x_prompt, x_sample, p_prompt, p_sample, w_in, out_norm_a, out_norm_b, na_rpb, w_out, ln1_g, ln1_b, w_router, b_router, w1, b1, w2, b2, ln2_g, ln2_b, w_ple, w_ple_gate, b_ple_gate, ple_norm_g):
    li = 0
    w_in_b = w_in[li].astype(BF16)
    wa, wb = w_in_b[:, :3 * D_A], w_in_b[:, 3 * D_A:]
    wo = w_out[li].astype(BF16)
    wr = w_router[li]
    wr_hi = wr.astype(BF16)
    wr_lo = (wr - wr_hi.astype(F32)).astype(BF16)
    wr2 = jnp.concatenate([wr_hi, wr_lo], axis=1)
    row = lambda v: v.reshape(1, -1).astype(F32)
    bias_a = _dilated_bias_tables()
    bias_b = _na_bias_tables(na_rpb[li])

    x1s, topis, topgs = [], [], []
    for x in (x_prompt, x_sample):
        b, t, _ = x.shape
        x2d = x.reshape(b * t, D_MODEL)
        qkv_a, qkv_b = _qkv_proj(x2d, wa, wb)
        ya = _attn_a(qkv_a.reshape(b, t, 3 * D_A), bias_a)
        yb = _attn_b(qkv_b.reshape(b, t, 3 * D_B), bias_b)
        x1, topi, topg = _outproj(ya.reshape(b * t, D_A), yb.reshape(b * t, D_B), x2d, wo,
                                  row(out_norm_a[li]), row(out_norm_b[li]), row(ln1_g[li]), row(ln1_b[li]),
                                  wr2, wr_hi, row(b_router[li]))
        x1s.append(x1)
        topis.append(topi)
        topgs.append(topg)

    bm = MOE_ROWS
    tm = 256
    pos, block_expert, n_valid, zero_blocks, n_blocks = _dispatch_tables(jnp.concatenate(topis, axis=0), bm)
    pos3 = pos.reshape(-1, 1, TOP_K * tm)
    pos3s, tile0 = [], 0
    for x1 in x1s:
        pos3s.append(pos3[tile0:tile0 + x1.shape[0] // tm])
        tile0 += x1.shape[0] // tm
    xs = _dispatch(zero_blocks, pos3, x1s[0], x1s[1], n_blocks * bm, bm, tm=tm)
    y_slots = _moe_ffn(block_expert, n_valid, xs,
                       w1[li].astype(BF16), b1[li].reshape(N_EXPERTS, 1, 2 * D_FF),
                       w2[li].astype(BF16), b2[li].reshape(N_EXPERTS, 1, D_MODEL), bm)

    wg = w_ple_gate[li].astype(BF16)
    wp = w_ple[li].astype(BF16)
    outs = []
    for x, p, x1, topg, pos3 in zip((x_prompt, x_sample), (p_prompt, p_sample), x1s, topgs, pos3s):
        b, t, _ = x.shape
        out = _final(pos3, y_slots, topg, x1, p[li].reshape(b * t, PLE_DIM),
                     row(ln2_g[li]), row(ln2_b[li]), wg, row(b_ple_gate[li]), wp, row(ple_norm_g[li]), tm=tm)
        outs.append(out.reshape(b, t, D_MODEL))
    return tuple(outs)
```

```python
import functools

import jax
import jax.numpy as jnp
import numpy as np
from jax import lax
from jax.experimental import pallas as pl
from jax.experimental.pallas import tpu as pltpu

F32 = jnp.float32
BF16 = jnp.bfloat16

D_MODEL = 2048
HEAD_DIM = 64
N_HEADS_A = 16
N_HEADS_B = 16
D_A = N_HEADS_A * HEAD_DIM
D_B = N_HEADS_B * HEAD_DIM
DILATIONS = (1, 4, 16)
BAND_HALF = 64
GRID_W = 64
NA_ROWS = 8
NA_COLS = 16
N_EXPERTS = 32
TOP_K = 4
D_FF = D_MODEL
SWIGLU_LIMIT = 7.0
SWIGLU_ALPHA = 1.702
PLE_DIM = 256
DEEPNORM_ALPHA = 2.0 ** 0.25
LN_EPS = 1e-5
RMS_EPS = 1e-6
NEG_INF = -1e30

LANES = 128
HEAD_PAIRS_A = D_A // LANES
HEAD_PAIRS_B = D_B // LANES
ATT_BLOCK = 1024
DIL_BLOCK = 2048
DIL_HALO = BAND_HALF * max(DILATIONS)
NA_GROUP_ROWS = 4
NA_UNION_ROWS = 12
NA_HALO = 256
MOE_ROWS = 512
MOE_FT = 1024
MOE_ROW_CHUNKS = 1
OUTPROJ_ROW_CHUNKS = 2
VMEM_LIMIT = 56 * 1024 * 1024


def _qkv_kernel(x_ref, wa_ref, wb_ref, oa_ref, ob_ref, xb_ref):
    @pl.when(pl.program_id(1) == 0)
    def _():
        xb_ref[...] = x_ref[...].astype(BF16)

    xb = xb_ref[...]
    oa_ref[...] = jnp.dot(xb, wa_ref[...], preferred_element_type=F32)
    ob_ref[...] = jnp.dot(xb, wb_ref[...], preferred_element_type=F32).astype(BF16)


def _qkv_proj(x2d, wa, wb, tm=1024, tn=512):
    n = x2d.shape[0]
    ca, cb = wa.shape[1], wb.shape[1]
    assert ca == cb and n % tm == 0 and ca % tn == 0
    return pl.pallas_call(
        _qkv_kernel,
        grid=(n // tm, ca // tn),
        in_specs=[
            pl.BlockSpec((tm, D_MODEL), lambda i, j: (i, 0)),
            pl.BlockSpec((D_MODEL, tn), lambda i, j: (0, j)),
            pl.BlockSpec((D_MODEL, tn), lambda i, j: (0, j)),
        ],
        out_specs=[
            pl.BlockSpec((tm, tn), lambda i, j: (i, j)),
            pl.BlockSpec((tm, tn), lambda i, j: (i, j)),
        ],
        out_shape=[jax.ShapeDtypeStruct((n, ca), F32), jax.ShapeDtypeStruct((n, cb), BF16)],
        scratch_shapes=[pltpu.VMEM((tm, D_MODEL), BF16)],
        compiler_params=pltpu.CompilerParams(
            dimension_semantics=("parallel", "arbitrary"), vmem_limit_bytes=VMEM_LIMIT),
        name="qkv_proj",
    )(x2d, wa, wb)


def _dilated_bias_tables():
    slopes = 2.0 ** (-8.0 * jnp.arange(1, N_HEADS_A + 1, dtype=F32) / N_HEADS_A)
    qi = jnp.arange(128)[:, None]
    kj = jnp.arange(256)[None, :] - BAND_HALF
    dist = jnp.abs(qi - kj)
    in_band = dist <= BAND_HALF
    tabs = []
    for d in DILATIONS:
        alibi = slopes[:, None, None] * (d * dist).astype(F32)[None]
        t = jnp.where(in_band[None], -alibi, NEG_INF)
        tabs.append(t.reshape(HEAD_PAIRS_A, 256, 256))
    return jnp.stack(tabs, axis=1)


def _attn_a_kernel(q_ref, kp_ref, kc_ref, kn_ref, vp_ref, vc_ref, vn_ref, bias_ref, o_ref,
                   kwin, vwin, acc_ref, m_ref, l_ref, *, seq_len):
    qb = pl.program_id(2)
    blk = DIL_BLOCK
    halo = DIL_HALO
    kwin[0:halo, :] = kp_ref[0]
    kwin[halo:halo + blk, :] = kc_ref[0]
    kwin[halo + blk:2 * halo + blk, :] = kn_ref[0]
    vwin[0:halo, :] = vp_ref[0]
    vwin[halo:halo + blk, :] = vc_ref[0]
    vwin[halo + blk:2 * halo + blk, :] = vn_ref[0]
    is_h0 = lax.broadcasted_iota(jnp.int32, (1, LANES), 1) < HEAD_DIM

    for p, d in enumerate(DILATIONS):
        lq = blk // d
        qs = 128
        ks = qs + 2 * BAND_HALF
        nsb = lq // qs
        sub_len = seq_len // d

        def body(it, carry, d=d, qs=qs, ks=ks, nsb=nsb, lq=lq, sub_len=sub_len, p=p):
            r = it // nsb
            sb = it % nsb
            bias = bias_ref[0, p, 0:2 * qs, 0:ks]
            col = lax.broadcasted_iota(jnp.int32, (1, ks), 1)
            q0 = r + d * sb * qs
            k0 = halo - BAND_HALF * d + q0
            if d == 1:
                q = q_ref[0, pl.ds(q0, qs), :]
                k = kwin[pl.ds(k0, ks), :]
                v = vwin[pl.ds(k0, ks), :]
            else:
                q = q_ref[0, pl.ds(q0, qs, stride=d), :]
                k = kwin[pl.ds(k0, ks, stride=d), :]
                v = vwin[pl.ds(k0, ks, stride=d), :]
            q = q * (HEAD_DIM ** -0.5)
            q2 = jnp.concatenate([jnp.where(is_h0, q, 0.0), jnp.where(is_h0, 0.0, q)], axis=0).astype(BF16)
            s = lax.dot_general(q2, k.astype(BF16), (((1,), (1,)), ((), ())),
                                preferred_element_type=F32)
            kidx = qb * lq + sb * qs - BAND_HALF + col
            ok = (kidx >= 0) & (kidx < sub_len)
            s = jnp.where(ok, s + bias, NEG_INF)
            m = jnp.max(s, axis=-1, keepdims=True)
            e = jnp.exp(s - m)
            l = jnp.sum(e, axis=-1, keepdims=True)
            o2 = jnp.dot(e.astype(BF16), v.astype(BF16), preferred_element_type=F32)
            o = jnp.where(is_h0, o2[:qs], o2[qs:])
            mm = jnp.where(is_h0, m[:qs], m[qs:])
            ll = jnp.where(is_h0, l[:qs], l[qs:])
            if d == 1:
                acc_ref[p, pl.ds(q0, qs), :] = o
                m_ref[p, pl.ds(q0, qs), :] = mm
                l_ref[p, pl.ds(q0, qs), :] = ll
            else:
                acc_ref[p, pl.ds(q0, qs, stride=d), :] = o
                m_ref[p, pl.ds(q0, qs, stride=d), :] = mm
                l_ref[p, pl.ds(q0, qs, stride=d), :] = ll
            return carry

        lax.fori_loop(0, d * nsb, body, 0, unroll=8)

    m_all = jnp.maximum(jnp.maximum(m_ref[0], m_ref[1]), m_ref[2])
    num = jnp.zeros((blk, LANES), F32)
    den = jnp.zeros((blk, LANES), F32)
    for p in range(len(DILATIONS)):
        a = jnp.exp(m_ref[p] - m_all)
        num = num + a * acc_ref[p]
        den = den + a * l_ref[p]
    o_ref[0] = num / den


def _attn_a(qkv_a, bias_tab):
    b, t, _ = qkv_a.shape
    blk = DIL_BLOCK
    halo = DIL_HALO
    nqb = t // blk
    assert t % blk == 0 and blk % halo == 0
    hp = HEAD_PAIRS_A
    per = blk // halo
    last = t // halo - 1

    def spec(col0, shift):
        if shift == 0:
            return pl.BlockSpec((1, blk, LANES), lambda h, bi, qi: (bi, qi, col0 + h))
        if shift < 0:
            return pl.BlockSpec((1, halo, LANES), lambda h, bi, qi: (bi, jnp.maximum(qi * per - 1, 0), col0 + h))
        return pl.BlockSpec((1, halo, LANES), lambda h, bi, qi: (bi, jnp.minimum((qi + 1) * per, last), col0 + h))

    return pl.pallas_call(
        functools.partial(_attn_a_kernel, seq_len=t),
        grid=(hp, b, nqb),
        in_specs=[
            spec(0, 0),
            spec(hp, -1), spec(hp, 0), spec(hp, 1),
            spec(2 * hp, -1), spec(2 * hp, 0), spec(2 * hp, 1),
            pl.BlockSpec((1, 3, 256, 256), lambda h, bi, qi: (h, 0, 0, 0)),
        ],
        out_specs=pl.BlockSpec((1, blk, LANES), lambda h, bi, qi: (bi, qi, h)),
        out_shape=jax.ShapeDtypeStruct((b, t, D_A), F32),
        scratch_shapes=[
            pltpu.VMEM((blk + 2 * halo, LANES), F32),
            pltpu.VMEM((blk + 2 * halo, LANES), F32),
            pltpu.VMEM((3, blk, LANES), F32),
            pltpu.VMEM((3, blk, LANES), F32),
            pltpu.VMEM((3, blk, LANES), F32),
        ],
        compiler_params=pltpu.CompilerParams(
            dimension_semantics=("parallel", "parallel", "parallel"), vmem_limit_bytes=VMEM_LIMIT),
        name="attn_dilated",
    )(qkv_a, qkv_a, qkv_a, qkv_a, qkv_a, qkv_a, qkv_a, bias_tab)


def _na_bias_tables(rpb):
    g, u = NA_GROUP_ROWS, NA_UNION_ROWS
    c = np.arange(GRID_W)
    col_start = np.clip(c - NA_COLS // 2, 0, GRID_W - NA_COLS)
    col_ok = (c[None, :] >= col_start[:, None]) & (c[None, :] < col_start[:, None] + NA_COLS)
    col_off = np.clip(c[None, :] - c[:, None], -(NA_COLS - 1), NA_COLS - 1) + (NA_COLS - 1)
    onehot = (col_off[None] == np.arange(2 * NA_COLS - 1)[:, None, None]).astype(np.float32)
    tab15 = jnp.einsum('hrc,cqk->hrqk', rpb.astype(F32), onehot, precision=lax.Precision.HIGHEST)
    tab15 = jnp.where(col_ok[None, None], tab15, NEG_INF)
    masked = jnp.full((N_HEADS_B, GRID_W, GRID_W), NEG_INF, F32)
    variants = []
    for r0_rel, delta0 in ((lambda rho: 0, 0),
                           (lambda rho: rho, -4),
                           (lambda rho: 4, -8)):
        q_rows = []
        for rho in range(g):
            tiles = []
            for kap in range(u):
                row_ok = r0_rel(rho) <= kap < r0_rel(rho) + NA_ROWS
                row_off = kap - rho + delta0 + NA_ROWS - 1
                tiles.append(tab15[:, row_off] if row_ok else masked)
            q_rows.append(jnp.concatenate(tiles, axis=-1))
        variants.append(jnp.concatenate(q_rows, axis=1))
    t = jnp.stack(variants, axis=1)
    t = t.reshape(HEAD_PAIRS_B, 2, 3, g * GRID_W, u * GRID_W).transpose(0, 2, 1, 3, 4)
    return t.reshape(HEAD_PAIRS_B, 3, 2 * g * GRID_W, u * GRID_W)


def _attn_b_kernel(q_ref, kp_ref, kc_ref, kn_ref, vp_ref, vc_ref, vn_ref, bias_ref, o_ref,
                   kwin, vwin, *, seq_len):
    qb = pl.program_id(2)
    blk = ATT_BLOCK
    halo = NA_HALO
    kwin[0:halo, :] = kp_ref[0]
    kwin[halo:halo + blk, :] = kc_ref[0]
    kwin[halo + blk:2 * halo + blk, :] = kn_ref[0]
    vwin[0:halo, :] = vp_ref[0]
    vwin[halo:halo + blk, :] = vc_ref[0]
    vwin[halo + blk:2 * halo + blk, :] = vn_ref[0]
    is_h0 = lax.broadcasted_iota(jnp.int32, (1, LANES), 1) < HEAD_DIM
    rows = seq_len // GRID_W
    rows_per_blk = blk // GRID_W
    gq = NA_GROUP_ROWS * GRID_W
    uk = NA_UNION_ROWS * GRID_W
    n_groups = rows_per_blk // NA_GROUP_ROWS

    def body(gi, carry):
        rg = qb * rows_per_blk + gi * NA_GROUP_ROWS
        u0 = jnp.clip(rg - NA_ROWS // 2, 0, rows - NA_UNION_ROWS)
        variant = jnp.where(rg == 0, 0, jnp.where(rg == rows - NA_GROUP_ROWS, 2, 1))
        start = pl.multiple_of((u0 - qb * rows_per_blk) * GRID_W + halo, GRID_W)
        q0 = pl.multiple_of(gi * gq, gq)
        q = q_ref[0, pl.ds(q0, gq), :].astype(F32) * (HEAD_DIM ** -0.5)
        q2 = jnp.concatenate([jnp.where(is_h0, q, 0.0), jnp.where(is_h0, 0.0, q)], axis=0).astype(BF16)
        k = kwin[pl.ds(start, uk), :]
        v = vwin[pl.ds(start, uk), :]
        s = lax.dot_general(q2, k, (((1,), (1,)), ((), ())), preferred_element_type=F32)
        s = s + bias_ref[0, variant]
        m = jnp.max(s, axis=-1, keepdims=True)
        e = jnp.exp(s - m)
        l = jnp.sum(e, axis=-1, keepdims=True)
        o2 = jnp.dot(e.astype(BF16), v, preferred_element_type=F32)
        o = jnp.where(is_h0, o2[:gq], o2[gq:])
        ll = jnp.where(is_h0, l[:gq], l[gq:])
        o_ref[0, pl.ds(q0, gq), :] = o / ll
        return carry

    lax.fori_loop(0, n_groups, body, 0, unroll=4)


def _attn_b(qkv_b, bias_tab):
    b, t, _ = qkv_b.shape
    blk = ATT_BLOCK
    halo = NA_HALO
    nqb = t // blk
    assert t % blk == 0 and blk % halo == 0 and (t // GRID_W) >= NA_UNION_ROWS
    hp = HEAD_PAIRS_B
    per = blk // halo
    last = t // halo - 1

    def spec(col0, shift):
        if shift == 0:
            return pl.BlockSpec((1, blk, LANES), lambda h, bi, qi: (bi, qi, col0 + h))
        if shift < 0:
            return pl.BlockSpec((1, halo, LANES), lambda h, bi, qi: (bi, jnp.maximum(qi * per - 1, 0), col0 + h))
        return pl.BlockSpec((1, halo, LANES), lambda h, bi, qi: (bi, jnp.minimum((qi + 1) * per, last), col0 + h))

    gq2 = 2 * NA_GROUP_ROWS * GRID_W
    uk = NA_UNION_ROWS * GRID_W
    return pl.pallas_call(
        functools.partial(_attn_b_kernel, seq_len=t),
        grid=(hp, b, nqb),
        in_specs=[
            spec(0, 0),
            spec(hp, -1), spec(hp, 0), spec(hp, 1),
            spec(2 * hp, -1), spec(2 * hp, 0), spec(2 * hp, 1),
            pl.BlockSpec((1, 3, gq2, uk), lambda h, bi, qi: (h, 0, 0, 0)),
        ],
        out_specs=pl.BlockSpec((1, blk, LANES), lambda h, bi, qi: (bi, qi, h)),
        out_shape=jax.ShapeDtypeStruct((b, t, D_B), F32),
        scratch_shapes=[
            pltpu.VMEM((blk + 2 * NA_HALO, LANES), BF16),
            pltpu.VMEM((blk + 2 * NA_HALO, LANES), BF16),
        ],
        compiler_params=pltpu.CompilerParams(
            dimension_semantics=("parallel", "parallel", "parallel"), vmem_limit_bytes=VMEM_LIMIT),
        name="attn_neighbourhood",
    )(qkv_b, qkv_b, qkv_b, qkv_b, qkv_b, qkv_b, qkv_b, bias_tab)


def _layer_norm(z, g, b):
    mu = jnp.mean(z, axis=-1, keepdims=True)
    zc = z - mu
    var = jnp.mean(zc * zc, axis=-1, keepdims=True)
    return zc * lax.rsqrt(var + LN_EPS) * g + b


def _rms_norm(y, g):
    ms = jnp.mean(y * y, axis=-1, keepdims=True)
    return y * lax.rsqrt(ms + RMS_EPS) * g


def _outproj_kernel(ya_ref, yb_ref, x_ref, wo_ref, ga_ref, gb_ref, lg_ref, lb_ref, wr2_ref, wrh_ref, br_ref,
                    x1_ref, topi_ref, topg_ref):
    chunk = x_ref.shape[0] // OUTPROJ_ROW_CHUNKS
    for c in range(OUTPROJ_ROW_CHUNKS):
        rows = pl.ds(c * chunk, chunk)
        na = _rms_norm(ya_ref[rows, :], ga_ref[...]).astype(BF16)
        nb = _rms_norm(yb_ref[rows, :], gb_ref[...]).astype(BF16)
        h = jnp.dot(na, wo_ref[0:D_A, :], preferred_element_type=F32)
        h = h + jnp.dot(nb, wo_ref[D_A:D_A + D_B, :], preferred_element_type=F32)
        x1 = _layer_norm(DEEPNORM_ALPHA * x_ref[rows, :] + h, lg_ref[...], lb_ref[...])
        x1_ref[rows, :] = x1
        hi = x1.astype(BF16)
        lo = (x1 - hi.astype(F32)).astype(BF16)
        l2 = jnp.dot(hi, wr2_ref[...], preferred_element_type=F32)
        logits = l2[:, :N_EXPERTS] + l2[:, N_EXPERTS:] + br_ref[...]
        logits = logits + jnp.dot(lo, wrh_ref[...], preferred_element_type=F32)
        lane = lax.broadcasted_iota(jnp.int32, logits.shape, 1)
        work = logits
        vals, idxs = [], []
        for _ in range(TOP_K):
            m = jnp.max(work, axis=-1, keepdims=True)
            idx = jnp.min(jnp.where(work == m, lane, N_EXPERTS), axis=-1, keepdims=True)
            vals.append(m)
            idxs.append(idx)
            work = jnp.where(lane == idx, -jnp.inf, work)
        es = [jnp.exp(v - vals[0]) for v in vals]
        tot = es[0] + es[1] + es[2] + es[3]
        topi_ref[rows, :] = jnp.concatenate(idxs, axis=1)
        topg_ref[rows, :] = jnp.concatenate([e / tot for e in es], axis=1)


def _outproj(ya2d, yb2d, x2d, wo, ga, gb, lg, lb, wr2, wrh, br, tm=512):
    n = x2d.shape[0]
    assert n % tm == 0
    row = lambda w: pl.BlockSpec((tm, w), lambda i: (i, 0))
    full = lambda a: pl.BlockSpec(a.shape, lambda i: (0,) * a.ndim, pipeline_mode=pl.Buffered(1))
    return pl.pallas_call(
        _outproj_kernel,
        grid=(n // tm,),
        in_specs=[row(D_A), row(D_B), row(D_MODEL), full(wo), full(ga), full(gb), full(lg), full(lb),
                  full(wr2), full(wrh), full(br)],
        out_specs=[row(D_MODEL), row(TOP_K), row(TOP_K)],
        out_shape=[jax.ShapeDtypeStruct((n, D_MODEL), F32),
                   jax.ShapeDtypeStruct((n, TOP_K), jnp.int32),
                   jax.ShapeDtypeStruct((n, TOP_K), F32)],
        compiler_params=pltpu.CompilerParams(
            dimension_semantics=("parallel",), vmem_limit_bytes=VMEM_LIMIT),
        name="outproj_ln_router",
    )(ya2d, yb2d, x2d, wo, ga, gb, lg, lb, wr2, wrh, br)


HALF_D = D_MODEL // 2
HI_MASK = np.uint32(0xFFFF0000)


def _dispatch_kernel(zb_ref, pos_ref, xa_ref, xb_ref, xs_hbm, pk_ref, zbuf, sem, zsem, *, tiles_a):
    i = pl.program_id(0)
    tm = xa_ref.shape[0]
    bm = zbuf.shape[0]

    @pl.when(i == 0)
    def _():
        zbuf[...] = jnp.zeros_like(zbuf)

        def zero_copy(z):
            start = pl.multiple_of(zb_ref[z] * bm, bm)
            return pltpu.make_async_copy(zbuf, xs_hbm.at[pl.ds(start, bm)], zsem)

        for z in range(2 * N_EXPERTS):
            @pl.when(zb_ref[z] >= 0)
            def _():
                zero_copy(z).start()
        for z in range(2 * N_EXPERTS):
            @pl.when(zb_ref[z] >= 0)
            def _():
                zero_copy(z).wait()

    def pack(x):
        bits = lax.bitcast_convert_type(x.astype(BF16).astype(F32), jnp.uint32)
        pk_ref[...] = (bits[:, :HALF_D] & HI_MASK) | (bits[:, HALF_D:] >> 16)

    @pl.when(i < tiles_a)
    def _():
        pack(xa_ref[...])

    @pl.when(i >= tiles_a)
    def _():
        pack(xb_ref[...])

    def issue(t, c):
        for k in range(TOP_K):
            pos = pos_ref[0, 0, TOP_K * t + k]
            pltpu.make_async_copy(pk_ref.at[t], xs_hbm.at[pos], sem).start()
        return c

    lax.fori_loop(0, tm, issue, 0)
    for k in range(TOP_K):
        pltpu.make_async_copy(pk_ref, xs_hbm.at[pl.ds(0, tm)], sem).wait()


def _dispatch(zero_blocks, pos3, x1a, x1b, n_slots, bm, tm=256):
    ta, tb = x1a.shape[0] // tm, x1b.shape[0] // tm
    assert x1a.shape[0] % tm == 0 and x1b.shape[0] % tm == 0
    grid_spec = pltpu.PrefetchScalarGridSpec(
        num_scalar_prefetch=1,
        grid=(ta + tb,),
        in_specs=[
            pl.BlockSpec((1, 1, TOP_K * tm), lambda i, zb: (i, 0, 0), memory_space=pltpu.SMEM),
            pl.BlockSpec((tm, D_MODEL), lambda i, zb: (jnp.minimum(i, ta - 1), 0)),
            pl.BlockSpec((tm, D_MODEL), lambda i, zb: (jnp.maximum(i - ta, 0), 0)),
        ],
        out_specs=pl.BlockSpec(memory_space=pl.ANY),
        scratch_shapes=[
            pltpu.VMEM((tm, HALF_D), jnp.uint32),
            pltpu.VMEM((bm, HALF_D), jnp.uint32),
            pltpu.SemaphoreType.DMA(()),
            pltpu.SemaphoreType.DMA(()),
        ],
    )
    return pl.pallas_call(
        functools.partial(_dispatch_kernel, tiles_a=ta),
        grid_spec=grid_spec,
        out_shape=jax.ShapeDtypeStruct((n_slots, HALF_D), jnp.uint32),
        compiler_params=pltpu.CompilerParams(
            dimension_semantics=("arbitrary",), vmem_limit_bytes=VMEM_LIMIT),
        name="moe_dispatch",
    )(zero_blocks, pos3, x1a, x1b)


def _moe_kernel(be_ref, nv_ref, xs_ref, w1g_ref, w1l_ref, b1g_ref, b1l_ref, w2_ref, b2_ref,
                out_ref, xsb_ref):
    i = pl.program_id(0)
    j = pl.program_id(1)
    valid = i < nv_ref[0]

    @pl.when(jnp.logical_and(valid, j == 0))
    def _():
        w = xs_ref[...]
        xsb_ref[:, :HALF_D] = lax.bitcast_convert_type(w & HI_MASK, F32).astype(BF16)
        xsb_ref[:, HALF_D:] = lax.bitcast_convert_type(w << 16, F32).astype(BF16)

    def ffn_rows(rows):
        xsb = xsb_ref[rows, :]
        glu = jnp.dot(xsb, w1g_ref[0], preferred_element_type=F32) + b1g_ref[0]
        lin = jnp.dot(xsb, w1l_ref[0], preferred_element_type=F32) + b1l_ref[0]
        glu = jnp.minimum(glu, SWIGLU_LIMIT)
        lin = jnp.clip(lin, -SWIGLU_LIMIT, SWIGLU_LIMIT)
        act = glu * jax.nn.sigmoid(SWIGLU_ALPHA * glu) * (lin + 1.0)
        return jnp.dot(act.astype(BF16), w2_ref[0], preferred_element_type=F32)

    chunk = xsb_ref.shape[0] // MOE_ROW_CHUNKS

    @pl.when(jnp.logical_and(valid, j == 0))
    def _():
        for c in range(MOE_ROW_CHUNKS):
            rows = pl.ds(c * chunk, chunk)
            out_ref[rows, :] = ffn_rows(rows) + b2_ref[0]

    @pl.when(jnp.logical_and(valid, j != 0))
    def _():
        for c in range(MOE_ROW_CHUNKS):
            rows = pl.ds(c * chunk, chunk)
            out_ref[rows, :] += ffn_rows(rows)

    @pl.when(jnp.logical_and(jnp.logical_not(valid), j == 0))
    def _():
        out_ref[...] = jnp.zeros_like(out_ref)


def _moe_ffn(block_expert, n_valid, xs, w1b, b1r, w2b, b2r, bm):
    n_blocks = xs.shape[0] // bm
    ft = MOE_FT
    nf = D_FF // ft
    grid_spec = pltpu.PrefetchScalarGridSpec(
        num_scalar_prefetch=2,
        grid=(n_blocks, nf),
        in_specs=[
            pl.BlockSpec((bm, HALF_D), lambda i, j, be, nv: (jnp.minimum(i, nv[0] - 1), 0)),
            pl.BlockSpec((1, D_MODEL, ft), lambda i, j, be, nv: (be[i], 0, j)),
            pl.BlockSpec((1, D_MODEL, ft), lambda i, j, be, nv: (be[i], 0, nf + j)),
            pl.BlockSpec((1, 1, ft), lambda i, j, be, nv: (be[i], 0, j)),
            pl.BlockSpec((1, 1, ft), lambda i, j, be, nv: (be[i], 0, nf + j)),
            pl.BlockSpec((1, ft, D_MODEL), lambda i, j, be, nv: (be[i], j, 0)),
            pl.BlockSpec((1, 1, D_MODEL), lambda i, j, be, nv: (be[i], 0, 0)),
        ],
        out_specs=pl.BlockSpec((bm, D_MODEL), lambda i, j, be, nv: (i, 0)),
        scratch_shapes=[pltpu.VMEM((bm, D_MODEL), BF16)],
    )
    return pl.pallas_call(
        _moe_kernel,
        grid_spec=grid_spec,
        out_shape=jax.ShapeDtypeStruct((n_blocks * bm, D_MODEL), F32),
        compiler_params=pltpu.CompilerParams(
            dimension_semantics=("arbitrary", "arbitrary"), vmem_limit_bytes=VMEM_LIMIT),
        name="moe_ffn",
    )(block_expert, n_valid, xs, w1b, w1b, b1r, b1r, w2b, b2r)


def _final_kernel(pos_ref, posn_ref, yb_hbm, gate_ref, x1_ref, p_ref, lg_ref, lb_ref, wg_ref, bg_ref, wp_ref,
                  pg_ref, out_ref, buf_a, buf_b, sem_a, sem_b):
    i = pl.program_id(0)
    last = pl.num_programs(0) - 1
    tm = x1_ref.shape[0]

    def row_copy(idx_ref, t, k, buf, sem):
        return pltpu.make_async_copy(yb_hbm.at[idx_ref[0, 0, TOP_K * t + k]], buf.at[k, t], sem)

    def wait_tile(buf, sem):
        for k in range(TOP_K):
            pltpu.make_async_copy(yb_hbm.at[pl.ds(0, tm)], buf.at[k], sem).wait()

    @pl.when(i == 0)
    def _():
        def issue(t, c):
            for k in range(TOP_K):
                row_copy(pos_ref, t, k, buf_a, sem_a).start()
            return c

        lax.fori_loop(0, tm, issue, 0)

    def step(cur, cur_sem, nxt, nxt_sem):
        wait_tile(cur, cur_sem)
        for t in range(tm):
            for k in range(TOP_K):
                row_copy(posn_ref, t, k, nxt, nxt_sem).start()
        g = gate_ref[...]
        y = cur[0] * g[:, 0:1]
        for k in range(1, TOP_K):
            y = y + cur[k] * g[:, k:k + 1]
        x2 = _layer_norm(DEEPNORM_ALPHA * x1_ref[...] + y, lg_ref[...], lb_ref[...])
        gate = jax.nn.sigmoid(jnp.dot(x2.astype(BF16), wg_ref[...], preferred_element_type=F32) + bg_ref[...])
        ple = jnp.dot(p_ref[...].astype(BF16), wp_ref[...], preferred_element_type=F32) * gate
        out_ref[...] = x2 + _rms_norm(ple, pg_ref[...])

        @pl.when(i == last)
        def _():
            wait_tile(nxt, nxt_sem)

    @pl.when(i % 2 == 0)
    def _():
        step(buf_a, sem_a, buf_b, sem_b)

    @pl.when(i % 2 == 1)
    def _():
        step(buf_b, sem_b, buf_a, sem_a)


def _final(pos3, yb, gates, x1, p2d, lg, lb, wg, bg, wp, pg, tm=256):
    n = p2d.shape[0]
    assert n % tm == 0
    nt = n // tm
    full = lambda a: pl.BlockSpec(a.shape, lambda i: (0,) * a.ndim)
    once = lambda a: pl.BlockSpec(a.shape, lambda i: (0,) * a.ndim, pipeline_mode=pl.Buffered(1))
    return pl.pallas_call(
        _final_kernel,
        grid=(nt,),
        in_specs=[
            pl.BlockSpec((1, 1, TOP_K * tm), lambda i: (i, 0, 0), memory_space=pltpu.SMEM),
            pl.BlockSpec((1, 1, TOP_K * tm), lambda i: (jnp.minimum(i + 1, nt - 1), 0, 0), memory_space=pltpu.SMEM),
            pl.BlockSpec(memory_space=pl.ANY),
            pl.BlockSpec((tm, TOP_K), lambda i: (i, 0)),
            pl.BlockSpec((tm, D_MODEL), lambda i: (i, 0)),
            pl.BlockSpec((tm, PLE_DIM), lambda i: (i, 0)),
            full(lg), full(lb), once(wg), full(bg), once(wp), full(pg),
        ],
        out_specs=pl.BlockSpec((tm, D_MODEL), lambda i: (i, 0)),
        out_shape=jax.ShapeDtypeStruct((n, D_MODEL), F32),
        scratch_shapes=[pltpu.VMEM((TOP_K, tm, D_MODEL), F32), pltpu.VMEM((TOP_K, tm, D_MODEL), F32),
                        pltpu.SemaphoreType.DMA(()), pltpu.SemaphoreType.DMA(())],
        compiler_params=pltpu.CompilerParams(
            dimension_semantics=("arbitrary",), vmem_limit_bytes=VMEM_LIMIT),
        name="combine_ln_ple",
    )(pos3, pos3, yb, gates, x1, p2d, lg, lb, wg, bg, wp, pg)


def _dispatch_tables(topi, bm):
    n = topi.shape[0]
    a = n * TOP_K
    flat_e = topi.reshape(-1)
    onehot = (flat_e[:, None] == jnp.arange(N_EXPERTS, dtype=jnp.int32)[None, :]).astype(jnp.int32)
    csum = jnp.cumsum(onehot, axis=0)
    counts = csum[-1]
    nblk_e = (counts + bm - 1) // bm
    blk_end = jnp.cumsum(nblk_e)
    blk_start = blk_end - nblk_e
    pos = jnp.sum(onehot * (csum - 1 + (blk_start * bm)[None, :]), axis=1)
    n_blocks = a // bm + N_EXPERTS
    blocks = jnp.arange(n_blocks, dtype=jnp.int32)
    block_expert = jnp.minimum(jnp.sum((blocks[:, None] >= blk_end[None, :]).astype(jnp.int32), axis=1),
                               N_EXPERTS - 1)
    n_valid = blk_end[-1:].astype(jnp.int32)
    last_blk = jnp.where(nblk_e > 0, blk_end - 1, -1)
    trailing = n_valid[0] + jnp.arange(N_EXPERTS, dtype=jnp.int32)
    trailing = jnp.where(trailing < n_blocks, trailing, -1)
    zero_blocks = jnp.concatenate([last_blk, trailing]).astype(jnp.int32)
    return pos.astype(jnp.int32), block_expert.astype(jnp.int32), n_valid, zero_blocks, n_blocks


def kernel(x_prompt, x_sample, p_prompt, p_sample, w_in, out_norm_a, out_norm_b, na_rpb, w_out, ln1_g, ln1_b, w_router, b_router, w1, b1, w2, b2, ln2_g, ln2_b, w_ple, w_ple_gate, b_ple_gate, ple_norm_g):
    li = 0
    w_in_b = w_in[li].astype(BF16)
    wa, wb = w_in_b[:, :3 * D_A], w_in_b[:, 3 * D_A:]
    wo = w_out[li].astype(BF16)
    wr = w_router[li]
    wr_hi = wr.astype(BF16)
    wr_lo = (wr - wr_hi.astype(F32)).astype(BF16)
    wr2 = jnp.concatenate([wr_hi, wr_lo], axis=1)
    row = lambda v: v.reshape(1, -1).astype(F32)
    bias_a = _dilated_bias_tables()
    bias_b = _na_bias_tables(na_rpb[li])

    x1s, topis, topgs = [], [], []
    for x in (x_prompt, x_sample):
        b, t, _ = x.shape
        x2d = x.reshape(b * t, D_MODEL)
        qkv_a, qkv_b = _qkv_proj(x2d, wa, wb)
        ya = _attn_a(qkv_a.reshape(b, t, 3 * D_A), bias_a)
        yb = _attn_b(qkv_b.reshape(b, t, 3 * D_B), bias_b)
        x1, topi, topg = _outproj(ya.reshape(b * t, D_A), yb.reshape(b * t, D_B), x2d, wo,
                                  row(out_norm_a[li]), row(out_norm_b[li]), row(ln1_g[li]), row(ln1_b[li]),
                                  wr2, wr_hi, row(b_router[li]))
        x1s.append(x1)
        topis.append(topi)
        topgs.append(topg)

    bm = MOE_ROWS
    tm = 256
    pos, block_expert, n_valid, zero_blocks, n_blocks = _dispatch_tables(jnp.concatenate(topis, axis=0), bm)
    pos3 = pos.reshape(-1, 1, TOP_K * tm)
    pos3s, tile0 = [], 0
    for x1 in x1s:
        pos3s.append(pos3[tile0:tile0 + x1.shape[0] // tm])
        tile0 += x1.shape[0] // tm
    xs = _dispatch(zero_blocks, pos3, x1s[0], x1s[1], n_blocks * bm, bm, tm=tm)
    y_slots = _moe_ffn(block_expert, n_valid, xs,
                       w1[li].astype(BF16), b1[li].reshape(N_EXPERTS, 1, 2 * D_FF),
                       w2[li].astype(BF16), b2[li].reshape(N_EXPERTS, 1, D_MODEL), bm)

    wg = w_ple_gate[li].astype(BF16)
    wp = w_ple[li].astype(BF16)
    outs = []
    for x, p, x1, topg, pos3 in zip((x_prompt, x_sample), (p_prompt, p_sample), x1s, topgs, pos3s):
        b, t, _ = x.shape
        out = _final(pos3, y_slots, topg, x1, p[li].reshape(b * t, PLE_DIM),
                     row(ln2_g[li]), row(ln2_b[li]), wg, row(b_ple_gate[li]), wp, row(ple_norm_g[li]), tm=tm)
        outs.append(out.reshape(b, t, D_MODEL))
    return tuple(outs)
```

```python
import functools

import jax
import jax.numpy as jnp
import numpy as np
from jax import lax
from jax.experimental import pallas as pl
from jax.experimental.pallas import tpu as pltpu

F32 = jnp.float32
BF16 = jnp.bfloat16

D_MODEL = 2048
HEAD_DIM = 64
N_HEADS_A = 16
N_HEADS_B = 16
D_A = N_HEADS_A * HEAD_DIM
D_B = N_HEADS_B * HEAD_DIM
DILATIONS = (1, 4, 16)
BAND_HALF = 64
GRID_W = 64
NA_ROWS = 8
NA_COLS = 16
N_EXPERTS = 32
TOP_K = 4
D_FF = D_MODEL
SWIGLU_LIMIT = 7.0
SWIGLU_ALPHA = 1.702
PLE_DIM = 256
DEEPNORM_ALPHA = 2.0 ** 0.25
LN_EPS = 1e-5
RMS_EPS = 1e-6
NEG_INF = -1e30

LANES = 128
HEAD_PAIRS_A = D_A // LANES
HEAD_PAIRS_B = D_B // LANES
ATT_BLOCK = 1024
DIL_BLOCK = 2048
DIL_HALO = BAND_HALF * max(DILATIONS)
NA_GROUP_ROWS = 4
NA_UNION_ROWS = 12
NA_HALO = 256
MOE_ROWS = 512
MOE_FT = 1024
OUTPROJ_ROW_CHUNKS = 2
VMEM_LIMIT = 56 * 1024 * 1024


def _qkv_kernel(x_ref, wa_ref, wb_ref, oa_ref, ob_ref, xb_ref):
    @pl.when(pl.program_id(1) == 0)
    def _():
        xb_ref[...] = x_ref[...].astype(BF16)

    xb = xb_ref[...]
    oa_ref[...] = jnp.dot(xb, wa_ref[...], preferred_element_type=F32)
    ob_ref[...] = jnp.dot(xb, wb_ref[...], preferred_element_type=F32).astype(BF16)


def _qkv_proj(x2d, wa, wb, tm=1024, tn=512):
    n = x2d.shape[0]
    ca, cb = wa.shape[1], wb.shape[1]
    assert ca == cb and n % tm == 0 and ca % tn == 0
    return pl.pallas_call(
        _qkv_kernel,
        grid=(n // tm, ca // tn),
        in_specs=[
            pl.BlockSpec((tm, D_MODEL), lambda i, j: (i, 0)),
            pl.BlockSpec((D_MODEL, tn), lambda i, j: (0, j)),
            pl.BlockSpec((D_MODEL, tn), lambda i, j: (0, j)),
        ],
        out_specs=[
            pl.BlockSpec((tm, tn), lambda i, j: (i, j)),
            pl.BlockSpec((tm, tn), lambda i, j: (i, j)),
        ],
        out_shape=[jax.ShapeDtypeStruct((n, ca), F32), jax.ShapeDtypeStruct((n, cb), BF16)],
        scratch_shapes=[pltpu.VMEM((tm, D_MODEL), BF16)],
        compiler_params=pltpu.CompilerParams(
            dimension_semantics=("parallel", "arbitrary"), vmem_limit_bytes=VMEM_LIMIT),
        name="qkv_proj",
    )(x2d, wa, wb)


def _dilated_bias_tables():
    slopes = 2.0 ** (-8.0 * jnp.arange(1, N_HEADS_A + 1, dtype=F32) / N_HEADS_A)
    qi = jnp.arange(128)[:, None]
    kj = jnp.arange(256)[None, :] - BAND_HALF
    dist = jnp.abs(qi - kj)
    in_band = dist <= BAND_HALF
    tabs = []
    for d in DILATIONS:
        alibi = slopes[:, None, None] * (d * dist).astype(F32)[None]
        t = jnp.where(in_band[None], -alibi, NEG_INF)
        tabs.append(t.reshape(HEAD_PAIRS_A, 256, 256))
    return jnp.stack(tabs, axis=1)


def _attn_a_kernel(q_ref, kp_ref, kc_ref, kn_ref, vp_ref, vc_ref, vn_ref, bias_ref, o_ref,
                   kwin, vwin, acc_ref, m_ref, l_ref, *, seq_len):
    qb = pl.program_id(2)
    blk = DIL_BLOCK
    halo = DIL_HALO
    kwin[0:halo, :] = kp_ref[0]
    kwin[halo:halo + blk, :] = kc_ref[0]
    kwin[halo + blk:2 * halo + blk, :] = kn_ref[0]
    vwin[0:halo, :] = vp_ref[0]
    vwin[halo:halo + blk, :] = vc_ref[0]
    vwin[halo + blk:2 * halo + blk, :] = vn_ref[0]
    is_h0 = lax.broadcasted_iota(jnp.int32, (1, LANES), 1) < HEAD_DIM

    for p, d in enumerate(DILATIONS):
        lq = blk // d
        qs = 128
        ks = qs + 2 * BAND_HALF
        nsb = lq // qs
        sub_len = seq_len // d

        def body(it, carry, d=d, qs=qs, ks=ks, nsb=nsb, lq=lq, sub_len=sub_len, p=p):
            r = it // nsb
            sb = it % nsb
            bias = bias_ref[0, p, 0:2 * qs, 0:ks]
            col = lax.broadcasted_iota(jnp.int32, (1, ks), 1)
            q0 = r + d * sb * qs
            k0 = halo - BAND_HALF * d + q0
            if d == 1:
                q = q_ref[0, pl.ds(q0, qs), :]
                k = kwin[pl.ds(k0, ks), :]
                v = vwin[pl.ds(k0, ks), :]
            else:
                q = q_ref[0, pl.ds(q0, qs, stride=d), :]
                k = kwin[pl.ds(k0, ks, stride=d), :]
                v = vwin[pl.ds(k0, ks, stride=d), :]
            q = q * (HEAD_DIM ** -0.5)
            q2 = jnp.concatenate([jnp.where(is_h0, q, 0.0), jnp.where(is_h0, 0.0, q)], axis=0).astype(BF16)
            s = lax.dot_general(q2, k.astype(BF16), (((1,), (1,)), ((), ())),
                                preferred_element_type=F32)
            kidx = qb * lq + sb * qs - BAND_HALF + col
            ok = (kidx >= 0) & (kidx < sub_len)
            s = jnp.where(ok, s + bias, NEG_INF)
            m = jnp.max(s, axis=-1, keepdims=True)
            e = jnp.exp(s - m)
            l = jnp.sum(e, axis=-1, keepdims=True)
            o2 = jnp.dot(e.astype(BF16), v.astype(BF16), preferred_element_type=F32)
            o = jnp.where(is_h0, o2[:qs], o2[qs:])
            mm = jnp.where(is_h0, m[:qs], m[qs:])
            ll = jnp.where(is_h0, l[:qs], l[qs:])
            if d == 1:
                acc_ref[p, pl.ds(q0, qs), :] = o
                m_ref[p, pl.ds(q0, qs), :] = mm
                l_ref[p, pl.ds(q0, qs), :] = ll
            else:
                acc_ref[p, pl.ds(q0, qs, stride=d), :] = o
                m_ref[p, pl.ds(q0, qs, stride=d), :] = mm
                l_ref[p, pl.ds(q0, qs, stride=d), :] = ll
            return carry

        lax.fori_loop(0, d * nsb, body, 0, unroll=8)

    m_all = jnp.maximum(jnp.maximum(m_ref[0], m_ref[1]), m_ref[2])
    num = jnp.zeros((blk, LANES), F32)
    den = jnp.zeros((blk, LANES), F32)
    for p in range(len(DILATIONS)):
        a = jnp.exp(m_ref[p] - m_all)
        num = num + a * acc_ref[p]
        den = den + a * l_ref[p]
    o_ref[0] = num / den


def _attn_a(qkv_a, bias_tab):
    b, t, _ = qkv_a.shape
    blk = DIL_BLOCK
    halo = DIL_HALO
    nqb = t // blk
    assert t % blk == 0 and blk % halo == 0
    hp = HEAD_PAIRS_A
    per = blk // halo
    last = t // halo - 1

    def spec(col0, shift):
        if shift == 0:
            return pl.BlockSpec((1, blk, LANES), lambda h, bi, qi: (bi, qi, col0 + h))
        if shift < 0:
            return pl.BlockSpec((1, halo, LANES), lambda h, bi, qi: (bi, jnp.maximum(qi * per - 1, 0), col0 + h))
        return pl.BlockSpec((1, halo, LANES), lambda h, bi, qi: (bi, jnp.minimum((qi + 1) * per, last), col0 + h))

    return pl.pallas_call(
        functools.partial(_attn_a_kernel, seq_len=t),
        grid=(hp, b, nqb),
        in_specs=[
            spec(0, 0),
            spec(hp, -1), spec(hp, 0), spec(hp, 1),
            spec(2 * hp, -1), spec(2 * hp, 0), spec(2 * hp, 1),
            pl.BlockSpec((1, 3, 256, 256), lambda h, bi, qi: (h, 0, 0, 0)),
        ],
        out_specs=pl.BlockSpec((1, blk, LANES), lambda h, bi, qi: (bi, qi, h)),
        out_shape=jax.ShapeDtypeStruct((b, t, D_A), F32),
        scratch_shapes=[
            pltpu.VMEM((blk + 2 * halo, LANES), F32),
            pltpu.VMEM((blk + 2 * halo, LANES), F32),
            pltpu.VMEM((3, blk, LANES), F32),
            pltpu.VMEM((3, blk, LANES), F32),
            pltpu.VMEM((3, blk, LANES), F32),
        ],
        compiler_params=pltpu.CompilerParams(
            dimension_semantics=("parallel", "parallel", "parallel"), vmem_limit_bytes=VMEM_LIMIT),
        name="attn_dilated",
    )(qkv_a, qkv_a, qkv_a, qkv_a, qkv_a, qkv_a, qkv_a, bias_tab)


def _na_bias_tables(rpb):
    g, u = NA_GROUP_ROWS, NA_UNION_ROWS
    c = np.arange(GRID_W)
    col_start = np.clip(c - NA_COLS // 2, 0, GRID_W - NA_COLS)
    col_ok = (c[None, :] >= col_start[:, None]) & (c[None, :] < col_start[:, None] + NA_COLS)
    col_off = np.clip(c[None, :] - c[:, None], -(NA_COLS - 1), NA_COLS - 1) + (NA_COLS - 1)
    onehot = (col_off[None] == np.arange(2 * NA_COLS - 1)[:, None, None]).astype(np.float32)
    tab15 = jnp.einsum('hrc,cqk->hrqk', rpb.astype(F32), onehot, precision=lax.Precision.HIGHEST)
    tab15 = jnp.where(col_ok[None, None], tab15, NEG_INF)
    masked = jnp.full((N_HEADS_B, GRID_W, GRID_W), NEG_INF, F32)
    variants = []
    for r0_rel, delta0 in ((lambda rho: 0, 0),
                           (lambda rho: rho, -4),
                           (lambda rho: 4, -8)):
        q_rows = []
        for rho in range(g):
            tiles = []
            for kap in range(u):
                row_ok = r0_rel(rho) <= kap < r0_rel(rho) + NA_ROWS
                row_off = kap - rho + delta0 + NA_ROWS - 1
                tiles.append(tab15[:, row_off] if row_ok else masked)
            q_rows.append(jnp.concatenate(tiles, axis=-1))
        variants.append(jnp.concatenate(q_rows, axis=1))
    t = jnp.stack(variants, axis=1)
    t = t.reshape(HEAD_PAIRS_B, 2, 3, g * GRID_W, u * GRID_W).transpose(0, 2, 1, 3, 4)
    return t.reshape(HEAD_PAIRS_B, 3, 2 * g * GRID_W, u * GRID_W)


def _attn_b_kernel(q_ref, kp_ref, kc_ref, kn_ref, vp_ref, vc_ref, vn_ref, bias_ref, o_ref,
                   kwin, vwin, *, seq_len):
    qb = pl.program_id(2)
    blk = ATT_BLOCK
    halo = NA_HALO
    kwin[0:halo, :] = kp_ref[0]
    kwin[halo:halo + blk, :] = kc_ref[0]
    kwin[halo + blk:2 * halo + blk, :] = kn_ref[0]
    vwin[0:halo, :] = vp_ref[0]
    vwin[halo:halo + blk, :] = vc_ref[0]
    vwin[halo + blk:2 * halo + blk, :] = vn_ref[0]
    is_h0 = lax.broadcasted_iota(jnp.int32, (1, LANES), 1) < HEAD_DIM
    rows = seq_len // GRID_W
    rows_per_blk = blk // GRID_W
    gq = NA_GROUP_ROWS * GRID_W
    uk = NA_UNION_ROWS * GRID_W
    n_groups = rows_per_blk // NA_GROUP_ROWS

    def body(gi, carry):
        rg = qb * rows_per_blk + gi * NA_GROUP_ROWS
        u0 = jnp.clip(rg - NA_ROWS // 2, 0, rows - NA_UNION_ROWS)
        variant = jnp.where(rg == 0, 0, jnp.where(rg == rows - NA_GROUP_ROWS, 2, 1))
        start = pl.multiple_of((u0 - qb * rows_per_blk) * GRID_W + halo, GRID_W)
        q0 = pl.multiple_of(gi * gq, gq)
        q = q_ref[0, pl.ds(q0, gq), :].astype(F32) * (HEAD_DIM ** -0.5)
        q2 = jnp.concatenate([jnp.where(is_h0, q, 0.0), jnp.where(is_h0, 0.0, q)], axis=0).astype(BF16)
        k = kwin[pl.ds(start, uk), :]
        v = vwin[pl.ds(start, uk), :]
        s = lax.dot_general(q2, k, (((1,), (1,)), ((), ())), preferred_element_type=F32)
        s = s + bias_ref[0, variant]
        m = jnp.max(s, axis=-1, keepdims=True)
        e = jnp.exp(s - m)
        l = jnp.sum(e, axis=-1, keepdims=True)
        o2 = jnp.dot(e.astype(BF16), v, preferred_element_type=F32)
        o = jnp.where(is_h0, o2[:gq], o2[gq:])
        ll = jnp.where(is_h0, l[:gq], l[gq:])
        o_ref[0, pl.ds(q0, gq), :] = o / ll
        return carry

    lax.fori_loop(0, n_groups, body, 0, unroll=4)


def _attn_b(qkv_b, bias_tab):
    b, t, _ = qkv_b.shape
    blk = ATT_BLOCK
    halo = NA_HALO
    nqb = t // blk
    assert t % blk == 0 and blk % halo == 0 and (t // GRID_W) >= NA_UNION_ROWS
    hp = HEAD_PAIRS_B
    per = blk // halo
    last = t // halo - 1

    def spec(col0, shift):
        if shift == 0:
            return pl.BlockSpec((1, blk, LANES), lambda h, bi, qi: (bi, qi, col0 + h))
        if shift < 0:
            return pl.BlockSpec((1, halo, LANES), lambda h, bi, qi: (bi, jnp.maximum(qi * per - 1, 0), col0 + h))
        return pl.BlockSpec((1, halo, LANES), lambda h, bi, qi: (bi, jnp.minimum((qi + 1) * per, last), col0 + h))

    gq2 = 2 * NA_GROUP_ROWS * GRID_W
    uk = NA_UNION_ROWS * GRID_W
    return pl.pallas_call(
        functools.partial(_attn_b_kernel, seq_len=t),
        grid=(hp, b, nqb),
        in_specs=[
            spec(0, 0),
            spec(hp, -1), spec(hp, 0), spec(hp, 1),
            spec(2 * hp, -1), spec(2 * hp, 0), spec(2 * hp, 1),
            pl.BlockSpec((1, 3, gq2, uk), lambda h, bi, qi: (h, 0, 0, 0)),
        ],
        out_specs=pl.BlockSpec((1, blk, LANES), lambda h, bi, qi: (bi, qi, h)),
        out_shape=jax.ShapeDtypeStruct((b, t, D_B), F32),
        scratch_shapes=[
            pltpu.VMEM((blk + 2 * NA_HALO, LANES), BF16),
            pltpu.VMEM((blk + 2 * NA_HALO, LANES), BF16),
        ],
        compiler_params=pltpu.CompilerParams(
            dimension_semantics=("parallel", "parallel", "parallel"), vmem_limit_bytes=VMEM_LIMIT),
        name="attn_neighbourhood",
    )(qkv_b, qkv_b, qkv_b, qkv_b, qkv_b, qkv_b, qkv_b, bias_tab)


def _layer_norm(z, g, b):
    mu = jnp.mean(z, axis=-1, keepdims=True)
    zc = z - mu
    var = jnp.mean(zc * zc, axis=-1, keepdims=True)
    return zc * lax.rsqrt(var + LN_EPS) * g + b


def _rms_norm(y, g):
    ms = jnp.mean(y * y, axis=-1, keepdims=True)
    return y * lax.rsqrt(ms + RMS_EPS) * g


def _outproj_kernel(ya_ref, yb_ref, x_ref, wo_ref, ga_ref, gb_ref, lg_ref, lb_ref, wr2_ref, wrh_ref, br_ref,
                    x1_ref, x1p_ref, topi_ref, topg_ref):
    chunk = x_ref.shape[0] // OUTPROJ_ROW_CHUNKS
    for c in range(OUTPROJ_ROW_CHUNKS):
        rows = pl.ds(c * chunk, chunk)
        na = _rms_norm(ya_ref[rows, :], ga_ref[...]).astype(BF16)
        nb = _rms_norm(yb_ref[rows, :], gb_ref[...]).astype(BF16)
        h = jnp.dot(na, wo_ref[0:D_A, :], preferred_element_type=F32)
        h = h + jnp.dot(nb, wo_ref[D_A:D_A + D_B, :], preferred_element_type=F32)
        x1 = _layer_norm(DEEPNORM_ALPHA * x_ref[rows, :] + h, lg_ref[...], lb_ref[...])
        x1_ref[rows, :] = x1
        hi = x1.astype(BF16)
        bits = lax.bitcast_convert_type(hi.astype(F32), jnp.uint32)
        x1p_ref[rows, :] = (bits[:, :HALF_D] & HI_MASK) | (bits[:, HALF_D:] >> 16)
        lo = (x1 - hi.astype(F32)).astype(BF16)
        l2 = jnp.dot(hi, wr2_ref[...], preferred_element_type=F32)
        logits = l2[:, :N_EXPERTS] + l2[:, N_EXPERTS:] + br_ref[...]
        logits = logits + jnp.dot(lo, wrh_ref[...], preferred_element_type=F32)
        lane = lax.broadcasted_iota(jnp.int32, logits.shape, 1)
        work = logits
        vals, idxs = [], []
        for _ in range(TOP_K):
            m = jnp.max(work, axis=-1, keepdims=True)
            idx = jnp.min(jnp.where(work == m, lane, N_EXPERTS), axis=-1, keepdims=True)
            vals.append(m)
            idxs.append(idx)
            work = jnp.where(lane == idx, -jnp.inf, work)
        es = [jnp.exp(v - vals[0]) for v in vals]
        tot = es[0] + es[1] + es[2] + es[3]
        topi_ref[rows, :] = jnp.concatenate(idxs, axis=1)
        topg_ref[rows, :] = jnp.concatenate([e / tot for e in es], axis=1)


def _outproj(ya2d, yb2d, x2d, wo, ga, gb, lg, lb, wr2, wrh, br, tm=512):
    n = x2d.shape[0]
    assert n % tm == 0
    row = lambda w: pl.BlockSpec((tm, w), lambda i: (i, 0))
    full = lambda a: pl.BlockSpec(a.shape, lambda i: (0,) * a.ndim, pipeline_mode=pl.Buffered(1))
    return pl.pallas_call(
        _outproj_kernel,
        grid=(n // tm,),
        in_specs=[row(D_A), row(D_B), row(D_MODEL), full(wo), full(ga), full(gb), full(lg), full(lb),
                  full(wr2), full(wrh), full(br)],
        out_specs=[row(D_MODEL), row(HALF_D), row(TOP_K), row(TOP_K)],
        out_shape=[jax.ShapeDtypeStruct((n, D_MODEL), F32),
                   jax.ShapeDtypeStruct((n, HALF_D), jnp.uint32),
                   jax.ShapeDtypeStruct((n, TOP_K), jnp.int32),
                   jax.ShapeDtypeStruct((n, TOP_K), F32)],
        compiler_params=pltpu.CompilerParams(
            dimension_semantics=("parallel",), vmem_limit_bytes=VMEM_LIMIT),
        name="outproj_ln_router",
    )(ya2d, yb2d, x2d, wo, ga, gb, lg, lb, wr2, wrh, br)


HALF_D = D_MODEL // 2
HI_MASK = np.uint32(0xFFFF0000)


def _moe_kernel(be_ref, nv_ref, tok_ref, tokn_ref, x_hbm, w1g_ref, w1l_ref, b1g_ref, b1l_ref, w2_ref, b2_ref,
                out_ref, xsb_ref, xg_a, xg_b, sem_a, sem_b):
    i = pl.program_id(0)
    j = pl.program_id(1)
    last_i = pl.num_programs(0) - 1
    last_j = pl.num_programs(1) - 1
    valid = i < nv_ref[0]
    even = i % 2 == 0
    bm = xsb_ref.shape[0]

    def row_copy(idx_ref, r, buf, sem):
        return pltpu.make_async_copy(x_hbm.at[idx_ref[0, 0, r]], buf.at[r], sem)

    def wait_block(buf, sem):
        pltpu.make_async_copy(x_hbm.at[pl.ds(0, bm)], buf, sem).wait()

    def issue_loop(idx_ref, buf, sem):
        def body(r, c):
            row_copy(idx_ref, r, buf, sem).start()
            return c

        lax.fori_loop(0, bm, body, 0)

    @pl.when(jnp.logical_and(i == 0, j == 0))
    def _():
        issue_loop(tok_ref, xg_a, sem_a)

    def take(buf, sem):
        wait_block(buf, sem)
        w = buf[...]
        xsb_ref[:, :HALF_D] = lax.bitcast_convert_type(w & HI_MASK, F32).astype(BF16)
        xsb_ref[:, HALF_D:] = lax.bitcast_convert_type(w << 16, F32).astype(BF16)

    @pl.when(jnp.logical_and(j == 0, even))
    def _():
        take(xg_a, sem_a)

    @pl.when(jnp.logical_and(j == 0, jnp.logical_not(even)))
    def _():
        take(xg_b, sem_b)

    def ffn_rows(rows):
        xsb = xsb_ref[rows, :]
        glu = jnp.dot(xsb, w1g_ref[0], preferred_element_type=F32) + b1g_ref[0]
        lin = jnp.dot(xsb, w1l_ref[0], preferred_element_type=F32) + b1l_ref[0]
        glu = jnp.minimum(glu, SWIGLU_LIMIT)
        lin = jnp.clip(lin, -SWIGLU_LIMIT, SWIGLU_LIMIT)
        act = glu * jax.nn.sigmoid(SWIGLU_ALPHA * glu) * (lin + 1.0)
        return jnp.dot(act.astype(BF16), w2_ref[0], preferred_element_type=F32)

    rows_all = pl.ds(0, bm)

    @pl.when(jnp.logical_and(valid, j == 0))
    def _():
        out_ref[...] = ffn_rows(rows_all) + b2_ref[0]

    if MOE_FT * 2 < D_FF:
        @pl.when(jnp.logical_and(valid, jnp.logical_and(j > 0, j < last_j)))
        def _():
            out_ref[...] += ffn_rows(rows_all)

    def last_tile(nxt, nxt_sem):
        for r in range(bm):
            row_copy(tokn_ref, r, nxt, nxt_sem).start()
        out_ref[...] += ffn_rows(rows_all)

        @pl.when(i == last_i)
        def _():
            wait_block(nxt, nxt_sem)

    @pl.when(jnp.logical_and(valid, jnp.logical_and(j == last_j, even)))
    def _():
        last_tile(xg_b, sem_b)

    @pl.when(jnp.logical_and(valid, jnp.logical_and(j == last_j, jnp.logical_not(even))))
    def _():
        last_tile(xg_a, sem_a)

    def idle_tile(nxt, nxt_sem):
        issue_loop(tokn_ref, nxt, nxt_sem)

        @pl.when(i == last_i)
        def _():
            wait_block(nxt, nxt_sem)

    @pl.when(jnp.logical_and(jnp.logical_not(valid), j == 0))
    def _():
        out_ref[...] = jnp.zeros_like(out_ref)

    @pl.when(jnp.logical_and(jnp.logical_not(valid), jnp.logical_and(j == last_j, even)))
    def _():
        idle_tile(xg_b, sem_b)

    @pl.when(jnp.logical_and(jnp.logical_not(valid), jnp.logical_and(j == last_j, jnp.logical_not(even))))
    def _():
        idle_tile(xg_a, sem_a)


def _moe_ffn(block_expert, n_valid, slot_tok, x1p, w1b, b1r, w2b, b2r):
    n_blocks, _, bm = slot_tok.shape
    ft = MOE_FT
    nf = D_FF // ft
    assert nf >= 2
    grid_spec = pltpu.PrefetchScalarGridSpec(
        num_scalar_prefetch=2,
        grid=(n_blocks, nf),
        in_specs=[
            pl.BlockSpec((1, 1, bm), lambda i, j, be, nv: (i, 0, 0), memory_space=pltpu.SMEM),
            pl.BlockSpec((1, 1, bm), lambda i, j, be, nv: (jnp.minimum(i + 1, n_blocks - 1), 0, 0),
                         memory_space=pltpu.SMEM),
            pl.BlockSpec(memory_space=pl.ANY),
            pl.BlockSpec((1, D_MODEL, ft), lambda i, j, be, nv: (be[i], 0, j)),
            pl.BlockSpec((1, D_MODEL, ft), lambda i, j, be, nv: (be[i], 0, nf + j)),
            pl.BlockSpec((1, 1, ft), lambda i, j, be, nv: (be[i], 0, j)),
            pl.BlockSpec((1, 1, ft), lambda i, j, be, nv: (be[i], 0, nf + j)),
            pl.BlockSpec((1, ft, D_MODEL), lambda i, j, be, nv: (be[i], j, 0)),
            pl.BlockSpec((1, 1, D_MODEL), lambda i, j, be, nv: (be[i], 0, 0)),
        ],
        out_specs=pl.BlockSpec((bm, D_MODEL), lambda i, j, be, nv: (i, 0)),
        scratch_shapes=[
            pltpu.VMEM((bm, D_MODEL), BF16),
            pltpu.VMEM((bm, HALF_D), jnp.uint32),
            pltpu.VMEM((bm, HALF_D), jnp.uint32),
            pltpu.SemaphoreType.DMA(()),
            pltpu.SemaphoreType.DMA(()),
        ],
    )
    return pl.pallas_call(
        _moe_kernel,
        grid_spec=grid_spec,
        out_shape=jax.ShapeDtypeStruct((n_blocks * bm, D_MODEL), F32),
        compiler_params=pltpu.CompilerParams(
            dimension_semantics=("arbitrary", "arbitrary"), vmem_limit_bytes=VMEM_LIMIT),
        name="moe_ffn",
    )(block_expert, n_valid, slot_tok, slot_tok, x1p, w1b, w1b, b1r, b1r, w2b, b2r)


def _final_kernel(pos_ref, posn_ref, yb_hbm, gate_ref, x1_ref, p_ref, lg_ref, lb_ref, wg_ref, bg_ref, wp_ref,
                  pg_ref, out_ref, buf_a, buf_b, sem_a, sem_b):
    i = pl.program_id(0)
    last = pl.num_programs(0) - 1
    tm = x1_ref.shape[0]

    def row_copy(idx_ref, t, k, buf, sem):
        return pltpu.make_async_copy(yb_hbm.at[idx_ref[0, 0, TOP_K * t + k]], buf.at[k, t], sem)

    def wait_tile(buf, sem):
        for k in range(TOP_K):
            pltpu.make_async_copy(yb_hbm.at[pl.ds(0, tm)], buf.at[k], sem).wait()

    @pl.when(i == 0)
    def _():
        def issue(t, c):
            for k in range(TOP_K):
                row_copy(pos_ref, t, k, buf_a, sem_a).start()
            return c

        lax.fori_loop(0, tm, issue, 0)

    def step(cur, cur_sem, nxt, nxt_sem):
        wait_tile(cur, cur_sem)
        for t in range(tm):
            for k in range(TOP_K):
                row_copy(posn_ref, t, k, nxt, nxt_sem).start()
        g = gate_ref[...]
        y = cur[0] * g[:, 0:1]
        for k in range(1, TOP_K):
            y = y + cur[k] * g[:, k:k + 1]
        x2 = _layer_norm(DEEPNORM_ALPHA * x1_ref[...] + y, lg_ref[...], lb_ref[...])
        gate = jax.nn.sigmoid(jnp.dot(x2.astype(BF16), wg_ref[...], preferred_element_type=F32) + bg_ref[...])
        ple = jnp.dot(p_ref[...].astype(BF16), wp_ref[...], preferred_element_type=F32) * gate
        out_ref[...] = x2 + _rms_norm(ple, pg_ref[...])

        @pl.when(i == last)
        def _():
            wait_tile(nxt, nxt_sem)

    @pl.when(i % 2 == 0)
    def _():
        step(buf_a, sem_a, buf_b, sem_b)

    @pl.when(i % 2 == 1)
    def _():
        step(buf_b, sem_b, buf_a, sem_a)


def _final(pos3, yb, gates, x1, p2d, lg, lb, wg, bg, wp, pg, tm=256):
    n = p2d.shape[0]
    assert n % tm == 0
    nt = n // tm
    full = lambda a: pl.BlockSpec(a.shape, lambda i: (0,) * a.ndim)
    once = lambda a: pl.BlockSpec(a.shape, lambda i: (0,) * a.ndim, pipeline_mode=pl.Buffered(1))
    return pl.pallas_call(
        _final_kernel,
        grid=(nt,),
        in_specs=[
            pl.BlockSpec((1, 1, TOP_K * tm), lambda i: (i, 0, 0), memory_space=pltpu.SMEM),
            pl.BlockSpec((1, 1, TOP_K * tm), lambda i: (jnp.minimum(i + 1, nt - 1), 0, 0), memory_space=pltpu.SMEM),
            pl.BlockSpec(memory_space=pl.ANY),
            pl.BlockSpec((tm, TOP_K), lambda i: (i, 0)),
            pl.BlockSpec((tm, D_MODEL), lambda i: (i, 0)),
            pl.BlockSpec((tm, PLE_DIM), lambda i: (i, 0)),
            full(lg), full(lb), once(wg), full(bg), once(wp), full(pg),
        ],
        out_specs=pl.BlockSpec((tm, D_MODEL), lambda i: (i, 0)),
        out_shape=jax.ShapeDtypeStruct((n, D_MODEL), F32),
        scratch_shapes=[pltpu.VMEM((TOP_K, tm, D_MODEL), F32), pltpu.VMEM((TOP_K, tm, D_MODEL), F32),
                        pltpu.SemaphoreType.DMA(()), pltpu.SemaphoreType.DMA(())],
        compiler_params=pltpu.CompilerParams(
            dimension_semantics=("arbitrary",), vmem_limit_bytes=VMEM_LIMIT),
        name="combine_ln_ple",
    )(pos3, pos3, yb, gates, x1, p2d, lg, lb, wg, bg, wp, pg)


def _dispatch_tables(topi, bm):
    n = topi.shape[0]
    a = n * TOP_K
    flat_e = topi.reshape(-1)
    onehot = (flat_e[:, None] == jnp.arange(N_EXPERTS, dtype=jnp.int32)[None, :]).astype(jnp.int32)
    csum = jnp.cumsum(onehot, axis=0)
    counts = csum[-1]
    nblk_e = (counts + bm - 1) // bm
    blk_end = jnp.cumsum(nblk_e)
    blk_start = blk_end - nblk_e
    pos = jnp.sum(onehot * (csum - 1 + (blk_start * bm)[None, :]), axis=1)
    n_blocks = a // bm + N_EXPERTS
    blocks = jnp.arange(n_blocks, dtype=jnp.int32)
    block_expert = jnp.minimum(jnp.sum((blocks[:, None] >= blk_end[None, :]).astype(jnp.int32), axis=1),
                               N_EXPERTS - 1)
    n_valid = blk_end[-1:].astype(jnp.int32)
    pos = pos.astype(jnp.int32)
    _, sorted_pair = lax.sort((pos, jnp.arange(a, dtype=jnp.int32)), num_keys=1)
    sorted_tok = jnp.concatenate([sorted_pair // TOP_K, jnp.zeros((bm,), jnp.int32)])
    group_start = jnp.cumsum(counts) - counts
    src0 = group_start[block_expert] + (blocks - blk_start[block_expert]) * bm
    src0 = jnp.clip(src0, 0, a)
    slot_tok = jax.vmap(lambda s: lax.dynamic_slice(sorted_tok, (s,), (bm,)))(src0)
    return pos, block_expert.astype(jnp.int32), n_valid, slot_tok.reshape(n_blocks, 1, bm).astype(jnp.int32)


def kernel(x_prompt, x_sample, p_prompt, p_sample, w_in, out_norm_a, out_norm_b, na_rpb, w_out, ln1_g, ln1_b, w_router, b_router, w1, b1, w2, b2, ln2_g, ln2_b, w_ple, w_ple_gate, b_ple_gate, ple_norm_g):
    li = 0
    w_in_b = w_in[li].astype(BF16)
    wa, wb = w_in_b[:, :3 * D_A], w_in_b[:, 3 * D_A:]
    wo = w_out[li].astype(BF16)
    wr = w_router[li]
    wr_hi = wr.astype(BF16)
    wr_lo = (wr - wr_hi.astype(F32)).astype(BF16)
    wr2 = jnp.concatenate([wr_hi, wr_lo], axis=1)
    row = lambda v: v.reshape(1, -1).astype(F32)
    bias_a = _dilated_bias_tables()
    bias_b = _na_bias_tables(na_rpb[li])

    x1s, x1ps, topis, topgs = [], [], [], []
    for x in (x_prompt, x_sample):
        b, t, _ = x.shape
        x2d = x.reshape(b * t, D_MODEL)
        qkv_a, qkv_b = _qkv_proj(x2d, wa, wb)
        ya = _attn_a(qkv_a.reshape(b, t, 3 * D_A), bias_a)
        yb = _attn_b(qkv_b.reshape(b, t, 3 * D_B), bias_b)
        x1, x1p, topi, topg = _outproj(ya.reshape(b * t, D_A), yb.reshape(b * t, D_B), x2d, wo,
                                       row(out_norm_a[li]), row(out_norm_b[li]), row(ln1_g[li]), row(ln1_b[li]),
                                       wr2, wr_hi, row(b_router[li]))
        x1s.append(x1)
        x1ps.append(x1p)
        topis.append(topi)
        topgs.append(topg)

    tm = 256
    pos, block_expert, n_valid, slot_tok = _dispatch_tables(jnp.concatenate(topis, axis=0), MOE_ROWS)
    pos3 = pos.reshape(-1, 1, TOP_K * tm)
    pos3s, tile0 = [], 0
    for x1 in x1s:
        pos3s.append(pos3[tile0:tile0 + x1.shape[0] // tm])
        tile0 += x1.shape[0] // tm
    y_slots = _moe_ffn(block_expert, n_valid, slot_tok, jnp.concatenate(x1ps, axis=0),
                       w1[li].astype(BF16), b1[li].reshape(N_EXPERTS, 1, 2 * D_FF),
                       w2[li].astype(BF16), b2[li].reshape(N_EXPERTS, 1, D_MODEL))

    wg = w_ple_gate[li].astype(BF16)
    wp = w_ple[li].astype(BF16)
    outs = []
    for x, p, x1, topg, pos3 in zip((x_prompt, x_sample), (p_prompt, p_sample), x1s, topgs, pos3s):
        b, t, _ = x.shape
        out = _final(pos3, y_slots, topg, x1, p[li].reshape(b * t, PLE_DIM),
                     row(ln2_g[li]), row(ln2_b[li]), wg, row(b_ple_gate[li]), wp, row(ple_norm_g[li]), tm=tm)
        outs.append(out.reshape(b, t, D_MODEL))
    return tuple(outs)
```

```python
import functools

import jax
import jax.numpy as jnp
import numpy as np
from jax import lax
from jax.experimental import pallas as pl
from jax.experimental.pallas import tpu as pltpu

F32 = jnp.float32
BF16 = jnp.bfloat16

D_MODEL = 2048
HEAD_DIM = 64
N_HEADS_A = 16
N_HEADS_B = 16
D_A = N_HEADS_A * HEAD_DIM
D_B = N_HEADS_B * HEAD_DIM
DILATIONS = (1, 4, 16)
BAND_HALF = 64
GRID_W = 64
NA_ROWS = 8
NA_COLS = 16
N_EXPERTS = 32
TOP_K = 4
D_FF = D_MODEL
SWIGLU_LIMIT = 7.0
SWIGLU_ALPHA = 1.702
PLE_DIM = 256
DEEPNORM_ALPHA = 2.0 ** 0.25
LN_EPS = 1e-5
RMS_EPS = 1e-6
NEG_INF = -1e30

LANES = 128
HEAD_PAIRS_A = D_A // LANES
HEAD_PAIRS_B = D_B // LANES
ATT_BLOCK = 1024
DIL_BLOCK = 2048
DIL_HALO = BAND_HALF * max(DILATIONS)
NA_GROUP_ROWS = 4
NA_UNION_ROWS = 12
NA_HALO = 256
MOE_ROWS = 512
MOE_FT = 1024
OUTPROJ_ROW_CHUNKS = 2
VMEM_LIMIT = 56 * 1024 * 1024


def _qkv_kernel(x_ref, wa_ref, wb_ref, oa_ref, ob_ref, xb_ref):
    @pl.when(pl.program_id(1) == 0)
    def _():
        xb_ref[...] = x_ref[...].astype(BF16)

    xb = xb_ref[...]
    oa_ref[...] = jnp.dot(xb, wa_ref[...], preferred_element_type=F32)
    ob_ref[...] = jnp.dot(xb, wb_ref[...], preferred_element_type=F32).astype(BF16)


def _qkv_proj(x2d, wa, wb, tm=1024, tn=512):
    n = x2d.shape[0]
    ca, cb = wa.shape[1], wb.shape[1]
    assert ca == cb and n % tm == 0 and ca % tn == 0
    return pl.pallas_call(
        _qkv_kernel,
        grid=(n // tm, ca // tn),
        in_specs=[
            pl.BlockSpec((tm, D_MODEL), lambda i, j: (i, 0)),
            pl.BlockSpec((D_MODEL, tn), lambda i, j: (0, j)),
            pl.BlockSpec((D_MODEL, tn), lambda i, j: (0, j)),
        ],
        out_specs=[
            pl.BlockSpec((tm, tn), lambda i, j: (i, j)),
            pl.BlockSpec((tm, tn), lambda i, j: (i, j)),
        ],
        out_shape=[jax.ShapeDtypeStruct((n, ca), F32), jax.ShapeDtypeStruct((n, cb), BF16)],
        scratch_shapes=[pltpu.VMEM((tm, D_MODEL), BF16)],
        compiler_params=pltpu.CompilerParams(
            dimension_semantics=("parallel", "arbitrary"), vmem_limit_bytes=VMEM_LIMIT),
        name="qkv_proj",
    )(x2d, wa, wb)


def _dilated_bias_tables():
    slopes = 2.0 ** (-8.0 * jnp.arange(1, N_HEADS_A + 1, dtype=F32) / N_HEADS_A)
    qi = jnp.arange(128)[:, None]
    kj = jnp.arange(256)[None, :] - BAND_HALF
    dist = jnp.abs(qi - kj)
    in_band = dist <= BAND_HALF
    tabs = []
    for d in DILATIONS:
        alibi = slopes[:, None, None] * (d * dist).astype(F32)[None]
        t = jnp.where(in_band[None], -alibi, NEG_INF)
        tabs.append(t.reshape(HEAD_PAIRS_A, 256, 256))
    return jnp.stack(tabs, axis=1)


def _attn_a_kernel(q_ref, kp_ref, kc_ref, kn_ref, vp_ref, vc_ref, vn_ref, bias_ref, o_ref,
                   kwin, vwin, acc_ref, m_ref, l_ref, *, seq_len):
    qb = pl.program_id(2)
    blk = DIL_BLOCK
    halo = DIL_HALO
    kwin[0:halo, :] = kp_ref[0]
    kwin[halo:halo + blk, :] = kc_ref[0]
    kwin[halo + blk:2 * halo + blk, :] = kn_ref[0]
    vwin[0:halo, :] = vp_ref[0]
    vwin[halo:halo + blk, :] = vc_ref[0]
    vwin[halo + blk:2 * halo + blk, :] = vn_ref[0]
    is_h0 = lax.broadcasted_iota(jnp.int32, (1, LANES), 1) < HEAD_DIM

    for p, d in enumerate(DILATIONS):
        lq = blk // d
        qs = 128
        ks = qs + 2 * BAND_HALF
        nsb = lq // qs
        sub_len = seq_len // d

        def body(it, carry, d=d, qs=qs, ks=ks, nsb=nsb, lq=lq, sub_len=sub_len, p=p):
            r = it // nsb
            sb = it % nsb
            bias = bias_ref[0, p, 0:2 * qs, 0:ks]
            col = lax.broadcasted_iota(jnp.int32, (1, ks), 1)
            q0 = r + d * sb * qs
            k0 = halo - BAND_HALF * d + q0
            if d == 1:
                q = q_ref[0, pl.ds(q0, qs), :]
                k = kwin[pl.ds(k0, ks), :]
                v = vwin[pl.ds(k0, ks), :]
            else:
                q = q_ref[0, pl.ds(q0, qs, stride=d), :]
                k = kwin[pl.ds(k0, ks, stride=d), :]
                v = vwin[pl.ds(k0, ks, stride=d), :]
            q = q * (HEAD_DIM ** -0.5)
            q2 = jnp.concatenate([jnp.where(is_h0, q, 0.0), jnp.where(is_h0, 0.0, q)], axis=0).astype(BF16)
            s = lax.dot_general(q2, k.astype(BF16), (((1,), (1,)), ((), ())),
                                preferred_element_type=F32)
            kidx = qb * lq + sb * qs - BAND_HALF + col
            ok = (kidx >= 0) & (kidx < sub_len)
            s = jnp.where(ok, s + bias, NEG_INF)
            m = jnp.max(s, axis=-1, keepdims=True)
            e = jnp.exp(s - m)
            l = jnp.sum(e, axis=-1, keepdims=True)
            o2 = jnp.dot(e.astype(BF16), v.astype(BF16), preferred_element_type=F32)
            o = jnp.where(is_h0, o2[:qs], o2[qs:])
            mm = jnp.where(is_h0, m[:qs], m[qs:])
            ll = jnp.where(is_h0, l[:qs], l[qs:])
            if d == 1:
                acc_ref[p, pl.ds(q0, qs), :] = o
                m_ref[p, pl.ds(q0, qs), :] = mm
                l_ref[p, pl.ds(q0, qs), :] = ll
            else:
                acc_ref[p, pl.ds(q0, qs, stride=d), :] = o
                m_ref[p, pl.ds(q0, qs, stride=d), :] = mm
                l_ref[p, pl.ds(q0, qs, stride=d), :] = ll
            return carry

        lax.fori_loop(0, d * nsb, body, 0, unroll=8)

    m_all = jnp.maximum(jnp.maximum(m_ref[0], m_ref[1]), m_ref[2])
    num = jnp.zeros((blk, LANES), F32)
    den = jnp.zeros((blk, LANES), F32)
    for p in range(len(DILATIONS)):
        a = jnp.exp(m_ref[p] - m_all)
        num = num + a * acc_ref[p]
        den = den + a * l_ref[p]
    o_ref[0] = num / den


def _attn_a(qkv_a, bias_tab):
    b, t, _ = qkv_a.shape
    blk = DIL_BLOCK
    halo = DIL_HALO
    nqb = t // blk
    assert t % blk == 0 and blk % halo == 0
    hp = HEAD_PAIRS_A
    per = blk // halo
    last = t // halo - 1

    def spec(col0, shift):
        if shift == 0:
            return pl.BlockSpec((1, blk, LANES), lambda h, bi, qi: (bi, qi, col0 + h))
        if shift < 0:
            return pl.BlockSpec((1, halo, LANES), lambda h, bi, qi: (bi, jnp.maximum(qi * per - 1, 0), col0 + h))
        return pl.BlockSpec((1, halo, LANES), lambda h, bi, qi: (bi, jnp.minimum((qi + 1) * per, last), col0 + h))

    return pl.pallas_call(
        functools.partial(_attn_a_kernel, seq_len=t),
        grid=(hp, b, nqb),
        in_specs=[
            spec(0, 0),
            spec(hp, -1), spec(hp, 0), spec(hp, 1),
            spec(2 * hp, -1), spec(2 * hp, 0), spec(2 * hp, 1),
            pl.BlockSpec((1, 3, 256, 256), lambda h, bi, qi: (h, 0, 0, 0)),
        ],
        out_specs=pl.BlockSpec((1, blk, LANES), lambda h, bi, qi: (bi, qi, h)),
        out_shape=jax.ShapeDtypeStruct((b, t, D_A), F32),
        scratch_shapes=[
            pltpu.VMEM((blk + 2 * halo, LANES), F32),
            pltpu.VMEM((blk + 2 * halo, LANES), F32),
            pltpu.VMEM((3, blk, LANES), F32),
            pltpu.VMEM((3, blk, LANES), F32),
            pltpu.VMEM((3, blk, LANES), F32),
        ],
        compiler_params=pltpu.CompilerParams(
            dimension_semantics=("parallel", "parallel", "parallel"), vmem_limit_bytes=VMEM_LIMIT),
        name="attn_dilated",
    )(qkv_a, qkv_a, qkv_a, qkv_a, qkv_a, qkv_a, qkv_a, bias_tab)


def _na_bias_tables(rpb):
    g, u = NA_GROUP_ROWS, NA_UNION_ROWS
    c = np.arange(GRID_W)
    col_start = np.clip(c - NA_COLS // 2, 0, GRID_W - NA_COLS)
    col_ok = (c[None, :] >= col_start[:, None]) & (c[None, :] < col_start[:, None] + NA_COLS)
    col_off = np.clip(c[None, :] - c[:, None], -(NA_COLS - 1), NA_COLS - 1) + (NA_COLS - 1)
    onehot = (col_off[None] == np.arange(2 * NA_COLS - 1)[:, None, None]).astype(np.float32)
    tab15 = jnp.einsum('hrc,cqk->hrqk', rpb.astype(F32), onehot, precision=lax.Precision.HIGHEST)
    tab15 = jnp.where(col_ok[None, None], tab15, NEG_INF)
    masked = jnp.full((N_HEADS_B, GRID_W, GRID_W), NEG_INF, F32)
    variants = []
    for r0_rel, delta0 in ((lambda rho: 0, 0),
                           (lambda rho: rho, -4),
                           (lambda rho: 4, -8)):
        q_rows = []
        for rho in range(g):
            tiles = []
            for kap in range(u):
                row_ok = r0_rel(rho) <= kap < r0_rel(rho) + NA_ROWS
                row_off = kap - rho + delta0 + NA_ROWS - 1
                tiles.append(tab15[:, row_off] if row_ok else masked)
            q_rows.append(jnp.concatenate(tiles, axis=-1))
        variants.append(jnp.concatenate(q_rows, axis=1))
    t = jnp.stack(variants, axis=1)
    t = t.reshape(HEAD_PAIRS_B, 2, 3, g * GRID_W, u * GRID_W).transpose(0, 2, 1, 3, 4)
    return t.reshape(HEAD_PAIRS_B, 3, 2 * g * GRID_W, u * GRID_W)


def _attn_b_kernel(q_ref, kp_ref, kc_ref, kn_ref, vp_ref, vc_ref, vn_ref, bias_ref, o_ref,
                   kwin, vwin, *, seq_len):
    qb = pl.program_id(2)
    blk = ATT_BLOCK
    halo = NA_HALO
    kwin[0:halo, :] = kp_ref[0]
    kwin[halo:halo + blk, :] = kc_ref[0]
    kwin[halo + blk:2 * halo + blk, :] = kn_ref[0]
    vwin[0:halo, :] = vp_ref[0]
    vwin[halo:halo + blk, :] = vc_ref[0]
    vwin[halo + blk:2 * halo + blk, :] = vn_ref[0]
    is_h0 = lax.broadcasted_iota(jnp.int32, (1, LANES), 1) < HEAD_DIM
    rows = seq_len // GRID_W
    rows_per_blk = blk // GRID_W
    gq = NA_GROUP_ROWS * GRID_W
    uk = NA_UNION_ROWS * GRID_W
    n_groups = rows_per_blk // NA_GROUP_ROWS

    def body(gi, carry):
        rg = qb * rows_per_blk + gi * NA_GROUP_ROWS
        u0 = jnp.clip(rg - NA_ROWS // 2, 0, rows - NA_UNION_ROWS)
        variant = jnp.where(rg == 0, 0, jnp.where(rg == rows - NA_GROUP_ROWS, 2, 1))
        start = pl.multiple_of((u0 - qb * rows_per_blk) * GRID_W + halo, GRID_W)
        q0 = pl.multiple_of(gi * gq, gq)
        q = q_ref[0, pl.ds(q0, gq), :].astype(F32) * (HEAD_DIM ** -0.5)
        q2 = jnp.concatenate([jnp.where(is_h0, q, 0.0), jnp.where(is_h0, 0.0, q)], axis=0).astype(BF16)
        k = kwin[pl.ds(start, uk), :]
        v = vwin[pl.ds(start, uk), :]
        s = lax.dot_general(q2, k, (((1,), (1,)), ((), ())), preferred_element_type=F32)
        s = s + bias_ref[0, variant]
        m = jnp.max(s, axis=-1, keepdims=True)
        e = jnp.exp(s - m)
        l = jnp.sum(e, axis=-1, keepdims=True)
        o2 = jnp.dot(e.astype(BF16), v, preferred_element_type=F32)
        o = jnp.where(is_h0, o2[:gq], o2[gq:])
        ll = jnp.where(is_h0, l[:gq], l[gq:])
        o_ref[0, pl.ds(q0, gq), :] = o / ll
        return carry

    lax.fori_loop(0, n_groups, body, 0, unroll=4)


def _attn_b(qkv_b, bias_tab):
    b, t, _ = qkv_b.shape
    blk = ATT_BLOCK
    halo = NA_HALO
    nqb = t // blk
    assert t % blk == 0 and blk % halo == 0 and (t // GRID_W) >= NA_UNION_ROWS
    hp = HEAD_PAIRS_B
    per = blk // halo
    last = t // halo - 1

    def spec(col0, shift):
        if shift == 0:
            return pl.BlockSpec((1, blk, LANES), lambda h, bi, qi: (bi, qi, col0 + h))
        if shift < 0:
            return pl.BlockSpec((1, halo, LANES), lambda h, bi, qi: (bi, jnp.maximum(qi * per - 1, 0), col0 + h))
        return pl.BlockSpec((1, halo, LANES), lambda h, bi, qi: (bi, jnp.minimum((qi + 1) * per, last), col0 + h))

    gq2 = 2 * NA_GROUP_ROWS * GRID_W
    uk = NA_UNION_ROWS * GRID_W
    return pl.pallas_call(
        functools.partial(_attn_b_kernel, seq_len=t),
        grid=(hp, b, nqb),
        in_specs=[
            spec(0, 0),
            spec(hp, -1), spec(hp, 0), spec(hp, 1),
            spec(2 * hp, -1), spec(2 * hp, 0), spec(2 * hp, 1),
            pl.BlockSpec((1, 3, gq2, uk), lambda h, bi, qi: (h, 0, 0, 0)),
        ],
        out_specs=pl.BlockSpec((1, blk, LANES), lambda h, bi, qi: (bi, qi, h)),
        out_shape=jax.ShapeDtypeStruct((b, t, D_B), F32),
        scratch_shapes=[
            pltpu.VMEM((blk + 2 * NA_HALO, LANES), BF16),
            pltpu.VMEM((blk + 2 * NA_HALO, LANES), BF16),
        ],
        compiler_params=pltpu.CompilerParams(
            dimension_semantics=("parallel", "parallel", "parallel"), vmem_limit_bytes=VMEM_LIMIT),
        name="attn_neighbourhood",
    )(qkv_b, qkv_b, qkv_b, qkv_b, qkv_b, qkv_b, qkv_b, bias_tab)


def _layer_norm(z, g, b):
    mu = jnp.mean(z, axis=-1, keepdims=True)
    zc = z - mu
    var = jnp.mean(zc * zc, axis=-1, keepdims=True)
    return zc * lax.rsqrt(var + LN_EPS) * g + b


def _rms_norm(y, g):
    ms = jnp.mean(y * y, axis=-1, keepdims=True)
    return y * lax.rsqrt(ms + RMS_EPS) * g


def _outproj_kernel(ya_ref, yb_ref, x_ref, wo_ref, ga_ref, gb_ref, lg_ref, lb_ref, wr2_ref, wrh_ref, br_ref,
                    x1_ref, topi_ref, topg_ref):
    chunk = x_ref.shape[0] // OUTPROJ_ROW_CHUNKS
    for c in range(OUTPROJ_ROW_CHUNKS):
        rows = pl.ds(c * chunk, chunk)
        na = _rms_norm(ya_ref[rows, :], ga_ref[...]).astype(BF16)
        nb = _rms_norm(yb_ref[rows, :], gb_ref[...]).astype(BF16)
        h = jnp.dot(na, wo_ref[0:D_A, :], preferred_element_type=F32)
        h = h + jnp.dot(nb, wo_ref[D_A:D_A + D_B, :], preferred_element_type=F32)
        x1 = _layer_norm(DEEPNORM_ALPHA * x_ref[rows, :] + h, lg_ref[...], lb_ref[...])
        x1_ref[rows, :] = x1
        hi = x1.astype(BF16)
        lo = (x1 - hi.astype(F32)).astype(BF16)
        l2 = jnp.dot(hi, wr2_ref[...], preferred_element_type=F32)
        logits = l2[:, :N_EXPERTS] + l2[:, N_EXPERTS:] + br_ref[...]
        logits = logits + jnp.dot(lo, wrh_ref[...], preferred_element_type=F32)
        lane = lax.broadcasted_iota(jnp.int32, logits.shape, 1)
        work = logits
        vals, idxs = [], []
        for _ in range(TOP_K):
            m = jnp.max(work, axis=-1, keepdims=True)
            idx = jnp.min(jnp.where(work == m, lane, N_EXPERTS), axis=-1, keepdims=True)
            vals.append(m)
            idxs.append(idx)
            work = jnp.where(lane == idx, -jnp.inf, work)
        es = [jnp.exp(v - vals[0]) for v in vals]
        tot = es[0] + es[1] + es[2] + es[3]
        topi_ref[rows, :] = jnp.concatenate(idxs, axis=1)
        topg_ref[rows, :] = jnp.concatenate([e / tot for e in es], axis=1)


def _outproj(ya2d, yb2d, x2d, wo, ga, gb, lg, lb, wr2, wrh, br, tm=512):
    n = x2d.shape[0]
    assert n % tm == 0
    row = lambda w: pl.BlockSpec((tm, w), lambda i: (i, 0))
    full = lambda a: pl.BlockSpec(a.shape, lambda i: (0,) * a.ndim, pipeline_mode=pl.Buffered(1))
    return pl.pallas_call(
        _outproj_kernel,
        grid=(n // tm,),
        in_specs=[row(D_A), row(D_B), row(D_MODEL), full(wo), full(ga), full(gb), full(lg), full(lb),
                  full(wr2), full(wrh), full(br)],
        out_specs=[row(D_MODEL), row(TOP_K), row(TOP_K)],
        out_shape=[jax.ShapeDtypeStruct((n, D_MODEL), F32),
                   jax.ShapeDtypeStruct((n, TOP_K), jnp.int32),
                   jax.ShapeDtypeStruct((n, TOP_K), F32)],
        compiler_params=pltpu.CompilerParams(
            dimension_semantics=("parallel",), vmem_limit_bytes=VMEM_LIMIT),
        name="outproj_ln_router",
    )(ya2d, yb2d, x2d, wo, ga, gb, lg, lb, wr2, wrh, br)


HALF_D = D_MODEL // 2
HI_MASK = np.uint32(0xFFFF0000)


def _pack_bf16_pairs(x):
    bits = lax.bitcast_convert_type(x.astype(BF16).astype(F32), jnp.uint32)
    return (bits[:, :HALF_D] & HI_MASK) | (bits[:, HALF_D:] >> 16)


def _unpack_bf16_pairs(w):
    return lax.bitcast_convert_type(w & HI_MASK, F32), lax.bitcast_convert_type(w << 16, F32)


def _dispatch_kernel(zb_ref, pos_ref, xa_ref, xb_ref, xs_hbm, pk_ref, zbuf, sem, zsem, *, tiles_a):
    i = pl.program_id(0)
    tm = xa_ref.shape[0]
    bm = zbuf.shape[0]

    @pl.when(i == 0)
    def _():
        zbuf[...] = jnp.zeros_like(zbuf)

        def zero_copy(z):
            start = pl.multiple_of(zb_ref[z] * bm, bm)
            return pltpu.make_async_copy(zbuf, xs_hbm.at[pl.ds(start, bm)], zsem)

        for z in range(2 * N_EXPERTS):
            @pl.when(zb_ref[z] >= 0)
            def _():
                zero_copy(z).start()
        for z in range(2 * N_EXPERTS):
            @pl.when(zb_ref[z] >= 0)
            def _():
                zero_copy(z).wait()

    @pl.when(i < tiles_a)
    def _():
        pk_ref[...] = _pack_bf16_pairs(xa_ref[...])

    @pl.when(i >= tiles_a)
    def _():
        pk_ref[...] = _pack_bf16_pairs(xb_ref[...])

    def issue(t, c):
        for k in range(TOP_K):
            pos = pos_ref[0, 0, TOP_K * t + k]
            pltpu.make_async_copy(pk_ref.at[t], xs_hbm.at[pos], sem).start()
        return c

    lax.fori_loop(0, tm, issue, 0)
    for k in range(TOP_K):
        pltpu.make_async_copy(pk_ref, xs_hbm.at[pl.ds(0, tm)], sem).wait()


def _dispatch(zero_blocks, pos3, x1a, x1b, n_slots, bm, tm=256):
    ta, tb = x1a.shape[0] // tm, x1b.shape[0] // tm
    assert x1a.shape[0] % tm == 0 and x1b.shape[0] % tm == 0
    grid_spec = pltpu.PrefetchScalarGridSpec(
        num_scalar_prefetch=1,
        grid=(ta + tb,),
        in_specs=[
            pl.BlockSpec((1, 1, TOP_K * tm), lambda i, zb: (i, 0, 0), memory_space=pltpu.SMEM),
            pl.BlockSpec((tm, D_MODEL), lambda i, zb: (jnp.minimum(i, ta - 1), 0)),
            pl.BlockSpec((tm, D_MODEL), lambda i, zb: (jnp.maximum(i - ta, 0), 0)),
        ],
        out_specs=pl.BlockSpec(memory_space=pl.ANY),
        scratch_shapes=[
            pltpu.VMEM((tm, HALF_D), jnp.uint32),
            pltpu.VMEM((bm, HALF_D), jnp.uint32),
            pltpu.SemaphoreType.DMA(()),
            pltpu.SemaphoreType.DMA(()),
        ],
    )
    return pl.pallas_call(
        functools.partial(_dispatch_kernel, tiles_a=ta),
        grid_spec=grid_spec,
        out_shape=jax.ShapeDtypeStruct((n_slots, HALF_D), jnp.uint32),
        compiler_params=pltpu.CompilerParams(
            dimension_semantics=("arbitrary",), vmem_limit_bytes=VMEM_LIMIT),
        name="moe_dispatch",
    )(zero_blocks, pos3, x1a, x1b)


def _moe_kernel(be_ref, nv_ref, xs_ref, w1g_ref, w1l_ref, b1g_ref, b1l_ref, w2_ref, b2_ref,
                out_ref, xsb_ref, acc_ref):
    i = pl.program_id(0)
    j = pl.program_id(1)
    last_j = pl.num_programs(1) - 1
    valid = i < nv_ref[0]
    bm = xsb_ref.shape[0]

    @pl.when(jnp.logical_and(valid, j == 0))
    def _():
        hi, lo = _unpack_bf16_pairs(xs_ref[...])
        xsb_ref[:, :HALF_D] = hi.astype(BF16)
        xsb_ref[:, HALF_D:] = lo.astype(BF16)

    def ffn_rows(rows):
        xsb = xsb_ref[rows, :]
        glu = jnp.dot(xsb, w1g_ref[0], preferred_element_type=F32) + b1g_ref[0]
        lin = jnp.dot(xsb, w1l_ref[0], preferred_element_type=F32) + b1l_ref[0]
        glu = jnp.minimum(glu, SWIGLU_LIMIT)
        lin = jnp.clip(lin, -SWIGLU_LIMIT, SWIGLU_LIMIT)
        act = glu * jax.nn.sigmoid(SWIGLU_ALPHA * glu) * (lin + 1.0)
        return jnp.dot(act.astype(BF16), w2_ref[0], preferred_element_type=F32)

    rows_all = pl.ds(0, bm)

    @pl.when(jnp.logical_and(valid, j == 0))
    def _():
        acc_ref[...] = ffn_rows(rows_all) + b2_ref[0]

    if MOE_FT * 2 < D_FF:
        @pl.when(jnp.logical_and(valid, jnp.logical_and(j > 0, j < last_j)))
        def _():
            acc_ref[...] += ffn_rows(rows_all)

    @pl.when(jnp.logical_and(valid, j == last_j))
    def _():
        out_ref[...] = _pack_bf16_pairs(acc_ref[...] + ffn_rows(rows_all))

    @pl.when(jnp.logical_and(jnp.logical_not(valid), j == 0))
    def _():
        out_ref[...] = jnp.zeros_like(out_ref)


def _moe_ffn(block_expert, n_valid, xs, w1b, b1r, w2b, b2r, bm):
    n_blocks = xs.shape[0] // bm
    ft = MOE_FT
    nf = D_FF // ft
    assert nf >= 2
    grid_spec = pltpu.PrefetchScalarGridSpec(
        num_scalar_prefetch=2,
        grid=(n_blocks, nf),
        in_specs=[
            pl.BlockSpec((bm, HALF_D), lambda i, j, be, nv: (jnp.minimum(i, nv[0] - 1), 0)),
            pl.BlockSpec((1, D_MODEL, ft), lambda i, j, be, nv: (be[i], 0, j)),
            pl.BlockSpec((1, D_MODEL, ft), lambda i, j, be, nv: (be[i], 0, nf + j)),
            pl.BlockSpec((1, 1, ft), lambda i, j, be, nv: (be[i], 0, j)),
            pl.BlockSpec((1, 1, ft), lambda i, j, be, nv: (be[i], 0, nf + j)),
            pl.BlockSpec((1, ft, D_MODEL), lambda i, j, be, nv: (be[i], j, 0)),
            pl.BlockSpec((1, 1, D_MODEL), lambda i, j, be, nv: (be[i], 0, 0)),
        ],
        out_specs=pl.BlockSpec((bm, HALF_D), lambda i, j, be, nv: (i, 0)),
        scratch_shapes=[pltpu.VMEM((bm, D_MODEL), BF16), pltpu.VMEM((bm, D_MODEL), F32)],
    )
    return pl.pallas_call(
        _moe_kernel,
        grid_spec=grid_spec,
        out_shape=jax.ShapeDtypeStruct((n_blocks * bm, HALF_D), jnp.uint32),
        compiler_params=pltpu.CompilerParams(
            dimension_semantics=("arbitrary", "arbitrary"), vmem_limit_bytes=VMEM_LIMIT),
        name="moe_ffn",
    )(block_expert, n_valid, xs, w1b, w1b, b1r, b1r, w2b, b2r)


def _final_kernel(pos_ref, posn_ref, yb_hbm, gate_ref, x1_ref, p_ref, lg_ref, lb_ref, wg_ref, bg_ref, wp_ref,
                  pg_ref, out_ref, buf_a, buf_b, sem_a, sem_b):
    i = pl.program_id(0)
    last = pl.num_programs(0) - 1
    tm = x1_ref.shape[0]

    def row_copy(idx_ref, t, k, buf, sem):
        return pltpu.make_async_copy(yb_hbm.at[idx_ref[0, 0, TOP_K * t + k]], buf.at[k, t], sem)

    def wait_tile(buf, sem):
        for k in range(TOP_K):
            pltpu.make_async_copy(yb_hbm.at[pl.ds(0, tm)], buf.at[k], sem).wait()

    @pl.when(i == 0)
    def _():
        def issue(t, c):
            for k in range(TOP_K):
                row_copy(pos_ref, t, k, buf_a, sem_a).start()
            return c

        lax.fori_loop(0, tm, issue, 0)

    def step(cur, cur_sem, nxt, nxt_sem):
        wait_tile(cur, cur_sem)
        for t in range(tm):
            for k in range(TOP_K):
                row_copy(posn_ref, t, k, nxt, nxt_sem).start()
        g = gate_ref[...]
        y_hi, y_lo = None, None
        for k in range(TOP_K):
            hi, lo = _unpack_bf16_pairs(cur[k])
            y_hi = hi * g[:, k:k + 1] if y_hi is None else y_hi + hi * g[:, k:k + 1]
            y_lo = lo * g[:, k:k + 1] if y_lo is None else y_lo + lo * g[:, k:k + 1]
        y = jnp.concatenate([y_hi, y_lo], axis=1)
        x2 = _layer_norm(DEEPNORM_ALPHA * x1_ref[...] + y, lg_ref[...], lb_ref[...])
        gate = jax.nn.sigmoid(jnp.dot(x2.astype(BF16), wg_ref[...], preferred_element_type=F32) + bg_ref[...])
        ple = jnp.dot(p_ref[...].astype(BF16), wp_ref[...], preferred_element_type=F32) * gate
        out_ref[...] = x2 + _rms_norm(ple, pg_ref[...])

        @pl.when(i == last)
        def _():
            wait_tile(nxt, nxt_sem)

    @pl.when(i % 2 == 0)
    def _():
        step(buf_a, sem_a, buf_b, sem_b)

    @pl.when(i % 2 == 1)
    def _():
        step(buf_b, sem_b, buf_a, sem_a)


def _final(pos3, yb, gates, x1, p2d, lg, lb, wg, bg, wp, pg, tm=256):
    n = p2d.shape[0]
    assert n % tm == 0
    nt = n // tm
    full = lambda a: pl.BlockSpec(a.shape, lambda i: (0,) * a.ndim)
    once = lambda a: pl.BlockSpec(a.shape, lambda i: (0,) * a.ndim, pipeline_mode=pl.Buffered(1))
    return pl.pallas_call(
        _final_kernel,
        grid=(nt,),
        in_specs=[
            pl.BlockSpec((1, 1, TOP_K * tm), lambda i: (i, 0, 0), memory_space=pltpu.SMEM),
            pl.BlockSpec((1, 1, TOP_K * tm), lambda i: (jnp.minimum(i + 1, nt - 1), 0, 0), memory_space=pltpu.SMEM),
            pl.BlockSpec(memory_space=pl.ANY),
            pl.BlockSpec((tm, TOP_K), lambda i: (i, 0)),
            pl.BlockSpec((tm, D_MODEL), lambda i: (i, 0)),
            pl.BlockSpec((tm, PLE_DIM), lambda i: (i, 0)),
            full(lg), full(lb), once(wg), full(bg), once(wp), full(pg),
        ],
        out_specs=pl.BlockSpec((tm, D_MODEL), lambda i: (i, 0)),
        out_shape=jax.ShapeDtypeStruct((n, D_MODEL), F32),
        scratch_shapes=[pltpu.VMEM((TOP_K, tm, HALF_D), jnp.uint32), pltpu.VMEM((TOP_K, tm, HALF_D), jnp.uint32),
                        pltpu.SemaphoreType.DMA(()), pltpu.SemaphoreType.DMA(())],
        compiler_params=pltpu.CompilerParams(
            dimension_semantics=("arbitrary",), vmem_limit_bytes=VMEM_LIMIT),
        name="combine_ln_ple",
    )(pos3, pos3, yb, gates, x1, p2d, lg, lb, wg, bg, wp, pg)


def _dispatch_tables(topi, bm):
    n = topi.shape[0]
    a = n * TOP_K
    flat_e = topi.reshape(-1)
    onehot = (flat_e[:, None] == jnp.arange(N_EXPERTS, dtype=jnp.int32)[None, :]).astype(jnp.int32)
    csum = jnp.cumsum(onehot, axis=0)
    counts = csum[-1]
    nblk_e = (counts + bm - 1) // bm
    blk_end = jnp.cumsum(nblk_e)
    blk_start = blk_end - nblk_e
    pos = jnp.sum(onehot * (csum - 1 + (blk_start * bm)[None, :]), axis=1)
    n_blocks = a // bm + N_EXPERTS
    blocks = jnp.arange(n_blocks, dtype=jnp.int32)
    block_expert = jnp.minimum(jnp.sum((blocks[:, None] >= blk_end[None, :]).astype(jnp.int32), axis=1),
                               N_EXPERTS - 1)
    n_valid = blk_end[-1:].astype(jnp.int32)
    last_blk = jnp.where(nblk_e > 0, blk_end - 1, -1)
    trailing = n_valid[0] + jnp.arange(N_EXPERTS, dtype=jnp.int32)
    trailing = jnp.where(trailing < n_blocks, trailing, -1)
    zero_blocks = jnp.concatenate([last_blk, trailing]).astype(jnp.int32)
    return pos.astype(jnp.int32), block_expert.astype(jnp.int32), n_valid, zero_blocks, n_blocks


def kernel(x_prompt, x_sample, p_prompt, p_sample, w_in, out_norm_a, out_norm_b, na_rpb, w_out, ln1_g, ln1_b, w_router, b_router, w1, b1, w2, b2, ln2_g, ln2_b, w_ple, w_ple_gate, b_ple_gate, ple_norm_g):
    li = 0
    w_in_b = w_in[li].astype(BF16)
    wa, wb = w_in_b[:, :3 * D_A], w_in_b[:, 3 * D_A:]
    wo = w_out[li].astype(BF16)
    wr = w_router[li]
    wr_hi = wr.astype(BF16)
    wr_lo = (wr - wr_hi.astype(F32)).astype(BF16)
    wr2 = jnp.concatenate([wr_hi, wr_lo], axis=1)
    row = lambda v: v.reshape(1, -1).astype(F32)
    bias_a = _dilated_bias_tables()
    bias_b = _na_bias_tables(na_rpb[li])

    x1s, topis, topgs = [], [], []
    for x in (x_prompt, x_sample):
        b, t, _ = x.shape
        x2d = x.reshape(b * t, D_MODEL)
        qkv_a, qkv_b = _qkv_proj(x2d, wa, wb)
        ya = _attn_a(qkv_a.reshape(b, t, 3 * D_A), bias_a)
        yb = _attn_b(qkv_b.reshape(b, t, 3 * D_B), bias_b)
        x1, topi, topg = _outproj(ya.reshape(b * t, D_A), yb.reshape(b * t, D_B), x2d, wo,
                                  row(out_norm_a[li]), row(out_norm_b[li]), row(ln1_g[li]), row(ln1_b[li]),
                                  wr2, wr_hi, row(b_router[li]))
        x1s.append(x1)
        topis.append(topi)
        topgs.append(topg)

    bm = MOE_ROWS
    tm = 256
    pos, block_expert, n_valid, zero_blocks, n_blocks = _dispatch_tables(jnp.concatenate(topis, axis=0), bm)
    pos3 = pos.reshape(-1, 1, TOP_K * tm)
    pos3s, tile0 = [], 0
    for x1 in x1s:
        pos3s.append(pos3[tile0:tile0 + x1.shape[0] // tm])
        tile0 += x1.shape[0] // tm
    xs = _dispatch(zero_blocks, pos3, x1s[0], x1s[1], n_blocks * bm, bm, tm=tm)
    y_slots = _moe_ffn(block_expert, n_valid, xs,
                       w1[li].astype(BF16), b1[li].reshape(N_EXPERTS, 1, 2 * D_FF),
                       w2[li].astype(BF16), b2[li].reshape(N_EXPERTS, 1, D_MODEL), bm)

    wg = w_ple_gate[li].astype(BF16)
    wp = w_ple[li].astype(BF16)
    outs = []
    for x, p, x1, topg, pos3 in zip((x_prompt, x_sample), (p_prompt, p_sample), x1s, topgs, pos3s):
        b, t, _ = x.shape
        out = _final(pos3, y_slots, topg, x1, p[li].reshape(b * t, PLE_DIM),
                     row(ln2_g[li]), row(ln2_b[li]), wg, row(b_ple_gate[li]), wp, row(ple_norm_g[li]), tm=tm)
        outs.append(out.reshape(b, t, D_MODEL))
    return tuple(outs)
```
